```python
import math
import jax
import jax.numpy as jnp
from jax import lax
import numpy as np

D_MODEL = 1024
BATCH = 16
SEQ = 2048
DEPTH = 4

CTX_LEN = 256
GRID_W = 64

FOURIER_WIDTH = D_MODEL // 4
FOURIER_GROUPS = 4
FOURIER_GROUP_DIM = FOURIER_WIDTH // FOURIER_GROUPS
POOL_WINDOWS = (2, 4, 8, 16)
POOL_WIDTH = D_MODEL // 4
POOL_GROUP_DIM = POOL_WIDTH // len(POOL_WINDOWS)
HEAD_DIM = 64
Q_WIDTH = D_MODEL // 2
N_HEADS = Q_WIDTH // HEAD_DIM
N_KV_HEADS = N_HEADS // 4
GQA_GROUP = N_HEADS // N_KV_HEADS
KV_WIDTH = N_KV_HEADS * HEAD_DIM
ROPE_THETA = 10000.0
ROPE_AX_FREQS = HEAD_DIM // 4
Q_BLOCK = 128
N_BRANCHES = 3
OFF_P = FOURIER_WIDTH
OFF_Q = OFF_P + POOL_WIDTH
OFF_K = OFF_Q + Q_WIDTH
OFF_V = OFF_K + KV_WIDTH
OFF_G = OFF_V + KV_WIDTH
IN_COLS = OFF_G + N_BRANCHES * D_MODEL
PEER_HEADS = 8
PEER_KEYS = 128
PEER_EXPERTS = PEER_KEYS * PEER_KEYS
PEER_TOPK = 16
PEER_KEY_DIM = 128
PEER_QDIM = 2 * PEER_KEY_DIM
PEER_CHUNK = 128
EPS = 1e-6

kernel_name = 'hybrid_fourier_pool_gqa_peer_dit'


def rmsnorm(x, g):
    x32 = x.astype(jnp.float32)
    y = x32 * lax.rsqrt(jnp.mean(x32 * x32, axis=-1, keepdims=True) + EPS)
    return (y * g.astype(jnp.float32)).astype(x.dtype)


def modulate(h, shift, scale):
    return h * (1 + scale) + shift


def axial_rope_angles(L):
    rows = L // GRID_W
    row = jnp.repeat(jnp.arange(rows), GRID_W).astype(jnp.float32)
    col = jnp.tile(jnp.arange(GRID_W), rows).astype(jnp.float32)
    inv = ROPE_THETA ** (-jnp.arange(ROPE_AX_FREQS, dtype=jnp.float32) / ROPE_AX_FREQS)
    return row[:, None] * inv, col[:, None] * inv


def rope_rotate(x, ang):
    F = ang.shape[-1]
    cos = jnp.cos(ang)[None, :, None, :].astype(x.dtype)
    sin = jnp.sin(ang)[None, :, None, :].astype(x.dtype)
    x1, x2 = x[..., :F], x[..., F:]
    return jnp.concatenate([x1 * cos - x2 * sin, x2 * cos + x1 * sin], axis=-1)


def apply_axial_rope(x, ang_row, ang_col):
    half = HEAD_DIM // 2
    return jnp.concatenate([rope_rotate(x[..., :half], ang_row),
                            rope_rotate(x[..., half:], ang_col)], axis=-1)


def split_proj(P):
    B, L, _ = P.shape
    f = P[..., :OFF_P]
    p = P[..., OFF_P:OFF_Q]
    q = P[..., OFF_Q:OFF_K].reshape(B, L, N_HEADS, HEAD_DIM)
    k = P[..., OFF_K:OFF_V].reshape(B, L, N_KV_HEADS, HEAD_DIM)
    v = P[..., OFF_V:OFF_G].reshape(B, L, N_KV_HEADS, HEAD_DIM)
    gates = jax.nn.sigmoid(P[..., OFF_G:]).reshape(B, L, N_BRANCHES, D_MODEL)
    return f, p, q, k, v, gates


def fourier_branch(f, w):
    B, L, _ = f.shape
    fg = f.reshape(B, L, FOURIER_GROUPS, FOURIER_GROUP_DIM).astype(jnp.float32)
    re = jnp.fft.fftn(fg, axes=(1, 3), norm='ortho').real
    return re.reshape(B, L, FOURIER_WIDTH).astype(f.dtype) @ w


def pool_branch(p, w_grp, scale, w_proj):
    B, L, _ = p.shape
    n_g = len(POOL_WINDOWS)
    pg = p.reshape(B, L, n_g, POOL_GROUP_DIM)
    cs = jnp.concatenate([jnp.zeros((B, 1, n_g, POOL_GROUP_DIM), jnp.float32),
                          jnp.cumsum(pg.astype(jnp.float32), axis=1)], axis=1)
    t = jnp.arange(L)
    means = []
    for gi, win in enumerate(POOL_WINDOWS):
        lo = jnp.maximum(t - win // 2, 0)
        hi = jnp.minimum(t + win - win // 2, L)
        s = cs[:, hi, gi] - cs[:, lo, gi]
        means.append(s / (hi - lo).astype(jnp.float32)[None, :, None])
    pooled = jnp.stack(means, axis=2).astype(p.dtype) - pg
    mixed = jnp.einsum('blgc,gce->blge', pooled, w_grp).reshape(B, L, POOL_WIDTH) * scale
    return mixed @ w_proj


def attend(q, k, v):
    s = jnp.einsum('bqhgd,bkhd->bhgqk', q, k).astype(jnp.float32) * (HEAD_DIM ** -0.5)
    p = jax.nn.softmax(s, axis=-1).astype(v.dtype)
    return jnp.einsum('bhgqk,bkhd->bqhgd', p, v)


def latent_attention(q, k_all, v_all):
    B, L = q.shape[0], q.shape[1]
    nb = L // Q_BLOCK
    qb = q.reshape(B, nb, Q_BLOCK, N_KV_HEADS, GQA_GROUP, HEAD_DIM).swapaxes(0, 1)
    ob = lax.map(lambda qq: attend(qq, k_all, v_all), qb)
    return ob.swapaxes(0, 1).reshape(B, L, Q_WIDTH)


def merge(yf, yp, ya, gates, w_out):
    return (gates[:, :, 0] * yf + gates[:, :, 1] * yp + gates[:, :, 2] * ya) @ w_out


def peer(h, wq, sub_keys, u_tab, v_tab):
    B, L, D = h.shape
    xs = h.reshape(-1, PEER_CHUNK, D)

    def block(xc):
        C = xc.shape[0]
        q = (xc @ wq).reshape(C, PEER_HEADS, 2, PEER_KEY_DIM)
        s = jnp.einsum('chpd,hpnd->chpn', q, sub_keys).astype(jnp.float32)
        s1, i1 = lax.top_k(s[:, :, 0], PEER_TOPK)
        s2, i2 = lax.top_k(s[:, :, 1], PEER_TOPK)
        cand = (s1[..., :, None] + s2[..., None, :]).reshape(C, PEER_HEADS, PEER_TOPK * PEER_TOPK)
        cidx = (i1[..., :, None] * PEER_KEYS + i2[..., None, :]).reshape(C, PEER_HEADS, PEER_TOPK * PEER_TOPK)
        top, pos = lax.top_k(cand, PEER_TOPK)
        eidx = jnp.take_along_axis(cidx, pos, axis=-1)
        g = jax.nn.softmax(top, axis=-1).astype(xc.dtype)
        a = jax.nn.gelu(jnp.einsum('chkd,cd->chk', u_tab[eidx], xc))
        return jnp.einsum('chk,chkd->cd', g * a, v_tab[eidx])

    return lax.map(block, xs).reshape(B, L, D)


def setup_inputs(seed: int = 0) -> dict:
    key = jax.random.key(seed)
    ks = jax.random.split(key, 24)
    f32 = jnp.float32

    def nrm(k, shape, scale):
        return jax.random.normal(k, shape, f32) * scale

    def gain(k, shape):
        return 1.0 + 0.02 * jax.random.normal(k, shape, f32)

    return {
        'x': nrm(ks[0], (BATCH, SEQ, D_MODEL), 1.0),
        'c': nrm(ks[1], (BATCH, D_MODEL), 1.0),
        'ctx': nrm(ks[2], (BATCH, CTX_LEN, D_MODEL), 1.0),
        'c_ctx': nrm(ks[3], (D_MODEL,), 1.0),
        'ada_w': nrm(ks[4], (DEPTH, D_MODEL, 6 * D_MODEL), 0.5 * D_MODEL ** -0.5),
        'ada_b': nrm(ks[5], (DEPTH, 6 * D_MODEL), 0.02),
        'norm_mix': gain(ks[6], (DEPTH, D_MODEL)),
        'w_in': nrm(ks[7], (DEPTH, D_MODEL, IN_COLS), D_MODEL ** -0.5),
        'fourier_w': nrm(ks[8], (DEPTH, FOURIER_WIDTH, D_MODEL), FOURIER_WIDTH ** -0.5),
        'pool_w': nrm(ks[9], (DEPTH, len(POOL_WINDOWS), POOL_GROUP_DIM, POOL_GROUP_DIM), POOL_GROUP_DIM ** -0.5),
        'pool_scale': gain(ks[10], (DEPTH, POOL_WIDTH)),
        'pool_proj': nrm(ks[11], (DEPTH, POOL_WIDTH, D_MODEL), POOL_WIDTH ** -0.5),
        'q_norm': gain(ks[12], (DEPTH, HEAD_DIM)),
        'k_norm': gain(ks[13], (DEPTH, HEAD_DIM)),
        'attn_proj': nrm(ks[14], (DEPTH, Q_WIDTH, D_MODEL), Q_WIDTH ** -0.5),
        'w_out': nrm(ks[15], (DEPTH, D_MODEL, D_MODEL), D_MODEL ** -0.5),
        'norm_ffn': gain(ks[16], (DEPTH, D_MODEL)),
        'peer_wq': nrm(ks[17], (DEPTH, D_MODEL, PEER_HEADS * PEER_QDIM), D_MODEL ** -0.5),
        'peer_keys': nrm(ks[18], (DEPTH, PEER_HEADS, 2, PEER_KEYS, PEER_KEY_DIM), PEER_KEY_DIM ** -0.5),
        'peer_u': nrm(ks[19], (DEPTH, PEER_EXPERTS, D_MODEL), D_MODEL ** -0.5),
        'peer_v': nrm(ks[20], (DEPTH, PEER_EXPERTS, D_MODEL), PEER_HEADS ** -0.5),
    }


def reference(x, c, ctx, c_ctx, ada_w, ada_b, norm_mix, w_in, fourier_w, pool_w, pool_scale,
              pool_proj, q_norm, k_norm, attn_proj, w_out, norm_ffn, peer_wq, peer_keys, peer_u, peer_v):
    B, L, _ = x.shape
    ang_row, ang_col = axial_rope_angles(L)
    for l in range(DEPTH):
        last = l == DEPTH - 1
        m_lat = jnp.split((jax.nn.silu(c) @ ada_w[l] + ada_b[l])[:, None, :], 6, axis=-1)
        m_ctx = jnp.split(jax.nn.silu(c_ctx) @ ada_w[l] + ada_b[l], 6, axis=-1)

        h = modulate(rmsnorm(x, norm_mix[l]), m_lat[0], m_lat[1])
        hc = modulate(rmsnorm(ctx, norm_mix[l]), m_ctx[0], m_ctx[1])
        f, p, q, k, v, gates = split_proj(h @ w_in[l])
        fc, pc, qc, kc, vc, gates_c = split_proj(hc @ w_in[l])

        q = apply_axial_rope(rmsnorm(q, q_norm[l]), ang_row, ang_col)
        k = apply_axial_rope(rmsnorm(k, k_norm[l]), ang_row, ang_col)
        kc = rmsnorm(kc, k_norm[l])
        k_all = jnp.concatenate([k, kc], axis=1)
        v_all = jnp.concatenate([v, vc], axis=1)
        q = q.reshape(B, L, N_KV_HEADS, GQA_GROUP, HEAD_DIM)
        ya = latent_attention(q, k_all, v_all) @ attn_proj[l]
        yf = fourier_branch(f, fourier_w[l])
        yp = pool_branch(p, pool_w[l], pool_scale[l], pool_proj[l])
        x = x + m_lat[2] * merge(yf, yp, ya, gates, w_out[l])

        if not last:
            Lc = ctx.shape[1]
            qc = rmsnorm(qc, q_norm[l]).reshape(B, Lc, N_KV_HEADS, GQA_GROUP, HEAD_DIM)
            yac = attend(qc, kc, vc).reshape(B, Lc, Q_WIDTH) @ attn_proj[l]
            yfc = fourier_branch(fc, fourier_w[l])
            ypc = pool_branch(pc, pool_w[l], pool_scale[l], pool_proj[l])
            ctx = ctx + m_ctx[2] * merge(yfc, ypc, yac, gates_c, w_out[l])

        h2 = modulate(rmsnorm(x, norm_ffn[l]), m_lat[3], m_lat[4])
        x = x + m_lat[5] * peer(h2, peer_wq[l], peer_keys[l], peer_u[l], peer_v[l])
        if not last:
            hc2 = modulate(rmsnorm(ctx, norm_ffn[l]), m_ctx[3], m_ctx[4])
            ctx = ctx + m_ctx[5] * peer(hc2, peer_wq[l], peer_keys[l], peer_u[l], peer_v[l])
    return x
```

```python
import functools
import math

import jax
import jax.numpy as jnp
from jax import lax
from jax.experimental import pallas as pl
from jax.experimental.pallas import tpu as pltpu

EPS = 1e-6
GRID_W = 64
HEAD_DIM = 64
GQA_GROUP = 4
ROPE_THETA = 10000.0
POOL_WINDOWS = (2, 4, 8, 16)
FOURIER_GROUPS = 4
PEER_HEADS = 8
PEER_KEYS = 128
PEER_TOPK = 16

LANES = 128
SUBLANES = 8
VMEM_LIMIT = 56 * 1024 * 1024

TOKEN_TILE = 256
ATTN_Q_TILE = 256
PEER_U_TILE = 128
PEER_V_TILE = 64

BF16 = jnp.bfloat16
F32 = jnp.float32
NEG_INF = float("-inf")


def _cparams(*sem):
    return pltpu.CompilerParams(dimension_semantics=sem, vmem_limit_bytes=VMEM_LIMIT)


def _rms(x, gain):
    return x * lax.rsqrt(jnp.mean(x * x, axis=-1, keepdims=True) + EPS) * gain


def _resident(shape):
    zeros = (0,) * len(shape)
    return pl.BlockSpec(shape, lambda *_: zeros)


def _adaln_kernel(c_ref, w_ref, b_ref, o_ref):
    c = c_ref[...]
    s = c * jax.nn.sigmoid(c)
    o_ref[0] = jnp.dot(s.astype(BF16), w_ref[0].astype(BF16), preferred_element_type=F32) + b_ref[0]


def _adaln(cvec, ada_w, ada_b):
    depth, d, six_d = ada_w.shape
    rows = cvec.shape[0]
    tn = 1024
    return pl.pallas_call(
        _adaln_kernel,
        grid=(depth, six_d // tn),
        in_specs=[pl.BlockSpec((rows, d), lambda l, j: (0, 0)),
                  pl.BlockSpec((1, d, tn), lambda l, j: (l, 0, j)),
                  pl.BlockSpec((1, 1, tn), lambda l, j: (l, 0, j))],
        out_specs=pl.BlockSpec((1, rows, tn), lambda l, j: (l, 0, j)),
        out_shape=jax.ShapeDtypeStruct((depth, rows, six_d), F32),
        compiler_params=_cparams("arbitrary", "arbitrary"),
        name="adaln",
    )(cvec, ada_w, ada_b.reshape(depth, 1, six_d))


def _proj_kernel(x_ref, mod_ref, g_ref, w_ref, dft_ref,
                 fg_ref, p_ref, q_ref, k_ref, v_ref, gate_ref, *, widths):
    fw, pw, qw, kw = widths
    x = x_ref[...]
    h = _rms(x, g_ref[...]) * (1.0 + mod_ref[0, 1:2, :]) + mod_ref[0, 0:1, :]
    hb = h.astype(BF16)

    def mm(c0, c1):
        return jnp.dot(hb, w_ref[:, c0:c1], preferred_element_type=F32)

    c = 0
    f = mm(c, c + fw); c += fw
    fg_ref[...] = jnp.dot(f.astype(BF16), dft_ref[...], preferred_element_type=F32).astype(BF16)
    p_ref[...] = mm(c, c + pw); c += pw
    q_ref[...] = mm(c, c + qw); c += qw
    k_ref[...] = mm(c, c + kw); c += kw
    v_ref[...] = mm(c, c + kw); c += kw
    n_gate = gate_ref.shape[1]
    step = 1024
    for j in range(0, n_gate, step):
        gate_ref[:, j:j + step] = jax.nn.sigmoid(mm(c + j, c + j + step))


def _in_proj(x, mods, gain, w_in_b, dft_c, tile_row, widths):
    t, d = x.shape
    fw, pw, qw, kw = widths
    n_gate = w_in_b.shape[1] - (fw + pw + qw + 2 * kw)
    tm = TOKEN_TILE
    row = lambda i: (i, 0)
    outs = [(2 * fw, BF16), (pw, F32), (qw, F32), (kw, F32), (kw, F32), (n_gate, F32)]
    return pl.pallas_call(
        functools.partial(_proj_kernel, widths=widths),
        grid=(t // tm,),
        in_specs=[pl.BlockSpec((tm, d), row),
                  pl.BlockSpec((1,) + mods.shape[1:], lambda i: (tile_row(i), 0, 0)),
                  _resident(gain.shape), _resident(w_in_b.shape), _resident(dft_c.shape)],
        out_specs=[pl.BlockSpec((tm, w), row) for w, _ in outs],
        out_shape=[jax.ShapeDtypeStruct((t, w), dt) for w, dt in outs],
        compiler_params=_cparams("arbitrary"),
        name="in_proj",
    )(x, mods, gain, w_in_b, dft_c)


def _mm_kernel(a_ref, b_ref, o_ref):
    o_ref[...] = jnp.dot(a_ref[...], b_ref[...], preferred_element_type=F32)


def _matmul(a, b, tm, tn):
    m, k = a.shape
    n = b.shape[1]
    return pl.pallas_call(
        _mm_kernel,
        grid=(m // tm, n // tn),
        in_specs=[pl.BlockSpec((tm, k), lambda i, j: (i, 0)),
                  pl.BlockSpec((k, tn), lambda i, j: (0, j))],
        out_specs=pl.BlockSpec((tm, tn), lambda i, j: (i, j)),
        out_shape=jax.ShapeDtypeStruct((m, n), F32),
        compiler_params=_cparams("arbitrary", "arbitrary"),
        name="seq_dft",
    )(a, b)


def _pool_kernel(p_ref, w_ref, s_ref, o_ref):
    x = p_ref[0]
    n, width = x.shape
    gdim = width // len(POOL_WINDOWS)
    t = lax.broadcasted_iota(jnp.int32, (n, 1), 0)
    col = lax.broadcasted_iota(jnp.int32, (1, width), 1)
    half = jnp.zeros((1, width), jnp.int32)
    for gi, win in enumerate(POOL_WINDOWS):
        half = jnp.where((col >= gi * gdim) & (col < (gi + 1) * gdim), win // 2, half)
    acc = jnp.zeros_like(x)
    max_half = max(POOL_WINDOWS) // 2
    for d in range(-max_half, max_half):
        shifted = x if d == 0 else pltpu.roll(x, (-d) % n, axis=0)
        row_ok = (t + d >= 0) & (t + d < n)
        col_ok = (half >= -d) if d < 0 else (half > d)
        acc = acc + jnp.where(row_ok, jnp.where(col_ok, shifted, 0.0), 0.0)
    cnt = (jnp.minimum(t + half, n) - jnp.maximum(t - half, 0)).astype(F32)
    pooled = acc / cnt - x
    o_ref[0] = jnp.dot(pooled.astype(BF16), w_ref[...], preferred_element_type=F32) * s_ref[...]


def _pool(p3, w_bd, scale):
    b, n, width = p3.shape
    blk = pl.BlockSpec((1, n, width), lambda i: (i, 0, 0))
    return pl.pallas_call(
        _pool_kernel,
        grid=(b,),
        in_specs=[blk, _resident(w_bd.shape), _resident(scale.shape)],
        out_specs=blk,
        out_shape=jax.ShapeDtypeStruct(p3.shape, F32),
        compiler_params=_cparams("arbitrary"),
        name="pool",
    )(p3, w_bd, scale)


def _swap_rope_halves(x):
    q = HEAD_DIM // 4
    return jnp.concatenate([x[:, q:2 * q], x[:, 0:q], x[:, 3 * q:4 * q], x[:, 2 * q:3 * q]], axis=1)


def _attn_kernel(q_ref, k_ref, v_ref, cos_ref, sin_ref, qn_ref, kn_ref, o_ref, kb_ref, vb_ref,
                 *, n_rope, tq):
    n_q = q_ref.shape[2]
    n_k = k_ref.shape[2]
    kn = _rms(k_ref[0, 0], kn_ref[...])
    if n_rope:
        k_lat = kn[:n_rope]
        kb_ref[0:n_rope, :] = (k_lat * cos_ref[...] + _swap_rope_halves(k_lat) * sin_ref[...]).astype(BF16)
        if n_k > n_rope:
            kb_ref[n_rope:n_k, :] = kn[n_rope:].astype(BF16)
    else:
        kb_ref[...] = kn.astype(BF16)
    vb_ref[...] = v_ref[0, 0].astype(BF16)
    n_tiles = n_q // tq

    def block(i, carry):
        g = i // n_tiles
        r0 = pl.multiple_of((i % n_tiles) * tq, tq)
        qn = _rms(q_ref[0, g, pl.ds(r0, tq), :], qn_ref[...])
        if n_rope:
            qn = qn * cos_ref[pl.ds(r0, tq), :] + _swap_rope_halves(qn) * sin_ref[pl.ds(r0, tq), :]
        qb = (qn * (HEAD_DIM ** -0.5)).astype(BF16)
        s = lax.dot_general(qb, kb_ref[...], (((1,), (1,)), ((), ())), preferred_element_type=F32)
        m = jnp.max(s, axis=-1, keepdims=True)
        p = jnp.exp(s - m)
        l = jnp.sum(p, axis=-1, keepdims=True)
        o = jnp.dot(p.astype(BF16), vb_ref[...], preferred_element_type=F32)
        o_ref[0, g, pl.ds(r0, tq), :] = o / l
        return carry

    lax.fori_loop(0, GQA_GROUP * n_tiles, block, 0)


def _attention(q4, k4, v4, cos, sin, q_gain, k_gain, n_rope):
    b, n_heads, n_q, hd = q4.shape
    n_kv = k4.shape[1]
    n_k = k4.shape[2]
    tq = min(ATTN_Q_TILE, n_q)
    qblk = pl.BlockSpec((1, GQA_GROUP, n_q, hd), lambda i, j: (i, j, 0, 0))
    kblk = pl.BlockSpec((1, 1, n_k, hd), lambda i, j: (i, j, 0, 0))
    return pl.pallas_call(
        functools.partial(_attn_kernel, n_rope=n_rope, tq=tq),
        grid=(b, n_kv),
        in_specs=[qblk, kblk, kblk, _resident(cos.shape), _resident(sin.shape),
                  _resident(q_gain.shape), _resident(k_gain.shape)],
        out_specs=qblk,
        out_shape=jax.ShapeDtypeStruct(q4.shape, F32),
        scratch_shapes=[pltpu.VMEM((n_k, hd), BF16), pltpu.VMEM((n_k, hd), BF16)],
        compiler_params=_cparams("arbitrary", "arbitrary"),
        name="attention",
    )(q4, k4, v4, cos, sin, q_gain, k_gain)


def _merge_kernel(re_ref, mx_ref, at_ref, gate_ref, x_ref, mod_ref, fw_ref, pw_ref, aw_ref, wo_ref, o_ref):
    d = x_ref.shape[1]
    yf = jnp.dot(re_ref[...].astype(BF16), fw_ref[...], preferred_element_type=F32)
    yp = jnp.dot(mx_ref[...].astype(BF16), pw_ref[...], preferred_element_type=F32)
    ya = jnp.dot(at_ref[...].astype(BF16), aw_ref[...], preferred_element_type=F32)
    z = gate_ref[:, 0:d] * yf + gate_ref[:, d:2 * d] * yp + gate_ref[:, 2 * d:3 * d] * ya
    y = jnp.dot(z.astype(BF16), wo_ref[...], preferred_element_type=F32)
    o_ref[...] = x_ref[...] + mod_ref[0, 2:3, :] * y


def _merge(re, mixed, attn, gates, x, mods, fw, pw, aw, wo, tile_row, n_tokens):
    d = x.shape[1]
    tm = TOKEN_TILE
    row = lambda i: (i, 0)
    ins = [re, mixed, attn, gates, x]
    return pl.pallas_call(
        _merge_kernel,
        grid=(n_tokens // tm,),
        in_specs=[pl.BlockSpec((tm, a.shape[1]), row) for a in ins]
        + [pl.BlockSpec((1,) + mods.shape[1:], lambda i: (tile_row(i), 0, 0))]
        + [_resident(w.shape) for w in (fw, pw, aw, wo)],
        out_specs=pl.BlockSpec((tm, d), row),
        out_shape=jax.ShapeDtypeStruct((n_tokens, d), F32),
        compiler_params=_cparams("arbitrary"),
        name="merge",
    )(*ins, mods, fw, pw, aw, wo)


def _topk_rows(s, k):
    n = s.shape[0]
    rows = lax.broadcasted_iota(jnp.int32, s.shape, 0)
    vals, ids = [], []
    for _ in range(k):
        m = jnp.max(s, axis=0, keepdims=True)
        r = jnp.min(jnp.where(s == m, rows, n), axis=0, keepdims=True)
        vals.append(m)
        ids.append(r)
        s = jnp.where(rows == r, NEG_INF, s)
    return jnp.concatenate(vals, axis=0), jnp.concatenate(ids, axis=0)


def _pair_candidates(s1, i1, s2, i2):
    k = PEER_TOPK
    row8 = lax.broadcasted_iota(jnp.int32, (SUBLANES, 1), 0)
    cand, expert = [s1[0:1] + s2], [i1[0:1] * PEER_KEYS + i2]
    for a in range(1, SUBLANES):
        nb = k // (a + 1)
        c = s1[a:a + 1] + s2[0:SUBLANES]
        cand.append(jnp.where(row8 < nb, c, NEG_INF))
        expert.append(i1[a:a + 1] * PEER_KEYS + i2[0:SUBLANES])
    cand.append(s1[SUBLANES:k] + s2[0:1])
    expert.append(i1[SUBLANES:k] * PEER_KEYS + i2[0:1])
    return jnp.concatenate(cand, axis=0), jnp.concatenate(expert, axis=0)


def _peer_q_kernel(x_ref, mod_ref, g_ref, wq_ref, keys_ref, h_ref, e_ref, gw_ref, q_scr):
    kd = keys_ref.shape[2]
    h = _rms(x_ref[...], g_ref[...]) * (1.0 + mod_ref[0, 4:5, :]) + mod_ref[0, 3:4, :]
    h_ref[...] = h
    hb = h.astype(BF16)
    for j in range(2 * PEER_HEADS):
        q_scr[j] = jnp.dot(hb, wq_ref[:, j * kd:(j + 1) * kd], preferred_element_type=F32).astype(BF16)

    def head(hh, carry):
        tops = []
        for p in range(2):
            s = lax.dot_general(keys_ref[2 * hh + p], q_scr[2 * hh + p], (((1,), (1,)), ((), ())),
                                preferred_element_type=F32)
            tops.append(_topk_rows(s, PEER_TOPK))
        cand, expert = _pair_candidates(tops[0][0], tops[0][1], tops[1][0], tops[1][1])
        n = cand.shape[0]
        rows = lax.broadcasted_iota(jnp.int32, cand.shape, 0)
        top, eid = [], []
        for _ in range(PEER_TOPK):
            m = jnp.max(cand, axis=0, keepdims=True)
            r = jnp.min(jnp.where(cand == m, rows, n), axis=0, keepdims=True)
            sel = rows == r
            top.append(m)
            eid.append(jnp.max(jnp.where(sel, expert, -1), axis=0, keepdims=True))
            cand = jnp.where(sel, NEG_INF, cand)
        top = jnp.concatenate(top, axis=0)
        ex = jnp.exp(top - top[0:1])
        r0 = pl.multiple_of(hh * PEER_TOPK, PEER_TOPK)
        gw_ref[pl.ds(r0, PEER_TOPK), :] = ex / jnp.sum(ex, axis=0, keepdims=True)
        e_ref[pl.ds(r0, PEER_TOPK), :] = jnp.concatenate(eid, axis=0)
        return carry

    lax.fori_loop(0, PEER_HEADS, head, 0)


def _peer_query(x, mods, gain, wq_b, keys_b, tile_row, n_tokens):
    d = x.shape[1]
    tm = TOKEN_TILE
    n_sel = PEER_HEADS * PEER_TOPK
    kd = keys_b.shape[2]
    return pl.pallas_call(
        _peer_q_kernel,
        grid=(n_tokens // tm,),
        in_specs=[pl.BlockSpec((tm, d), lambda i: (i, 0)),
                  pl.BlockSpec((1,) + mods.shape[1:], lambda i: (tile_row(i), 0, 0)),
                  _resident(gain.shape), _resident(wq_b.shape), _resident(keys_b.shape)],
        out_specs=[pl.BlockSpec((tm, d), lambda i: (i, 0)),
                   pl.BlockSpec((n_sel, tm), lambda i: (0, i)),
                   pl.BlockSpec((n_sel, tm), lambda i: (0, i))],
        out_shape=[jax.ShapeDtypeStruct((n_tokens, d), F32),
                   jax.ShapeDtypeStruct((n_sel, n_tokens), jnp.int32),
                   jax.ShapeDtypeStruct((n_sel, n_tokens), F32)],
        scratch_shapes=[pltpu.VMEM((2 * PEER_HEADS, tm, kd), BF16)],
        compiler_params=_cparams("arbitrary"),
        name="peer_query",
    )(x, mods, gain, wq_b, keys_b)


def _fold_pairs(vs, shift):
    sub = lax.broadcasted_iota(jnp.int32, (SUBLANES, LANES), 0)
    keep = (sub % (2 * shift)) < shift
    out = []
    for i in range(0, len(vs), 2):
        a = vs[i] + pltpu.roll(vs[i], SUBLANES - shift, axis=0)
        b = vs[i + 1] + pltpu.roll(vs[i + 1], SUBLANES - shift, axis=0)
        out.append(jnp.where(keep, a, pltpu.roll(b, shift, axis=0)))
    return out


def _trace_fold_rows():
    vs = [[i] * SUBLANES for i in range(SUBLANES)]
    for shift in (4, 2, 1):
        vs = [[vs[i][r] if (r % (2 * shift)) < shift else vs[i + 1][r - shift] for r in range(SUBLANES)]
              for i in range(0, len(vs), 2)]
    return vs[0]


_SLOT_PERM = _trace_fold_rows()


def _peer_u_kernel(idx_ref, x_ref, tab_ref, gw_ref, o_ref, a_scr):
    n_sel = idx_ref.shape[1]
    lane = lax.broadcasted_iota(jnp.int32, (n_sel, LANES), 1)
    a_scr[...] = jnp.zeros_like(a_scr)

    def token(t, carry):
        xt = x_ref[t]
        prods = [None] * n_sel
        for k in range(n_sel):
            f, s = divmod(k, SUBLANES)
            prods[f * SUBLANES + _SLOT_PERM[s]] = tab_ref[idx_ref[t, k]].astype(F32) * xt
        for shift in (4, 2, 1):
            prods = _fold_pairs(prods, shift)
        m = jnp.concatenate(prods, axis=0)
        col = jnp.sum(m, axis=1, keepdims=True)
        a_scr[...] = jnp.where(lane == t, col, a_scr[...])
        return carry

    lax.fori_loop(0, x_ref.shape[0], token, 0)
    o_ref[...] = gw_ref[...] * jax.nn.gelu(a_scr[...])


def _peer_u(idx_tm, h3, tab, gw):
    t = h3.shape[0]
    n_sel = idx_tm.shape[1]
    tg = PEER_U_TILE
    return pl.pallas_call(
        _peer_u_kernel,
        grid=(t // tg,),
        in_specs=[pl.BlockSpec((tg, n_sel), lambda i: (i, 0), memory_space=pltpu.SMEM),
                  pl.BlockSpec((tg,) + h3.shape[1:], lambda i: (i, 0, 0)),
                  pl.BlockSpec(tab.shape, lambda i: (0, 0, 0), pipeline_mode=pl.Buffered(1)),
                  pl.BlockSpec((n_sel, tg), lambda i: (0, i))],
        out_specs=pl.BlockSpec((n_sel, tg), lambda i: (0, i)),
        out_shape=jax.ShapeDtypeStruct((n_sel, t), F32),
        scratch_shapes=[pltpu.VMEM((n_sel, LANES), F32)],
        compiler_params=_cparams("arbitrary"),
        name="peer_u",
    )(idx_tm, h3, tab, gw)


def _peer_v_kernel(idx_ref, w_ref, tab_ref, o_ref):
    n_sel = idx_ref.shape[1]
    n_acc = 4

    def token(t, carry):
        acc = [None] * n_acc
        for k in range(n_sel):
            term = tab_ref[idx_ref[t, k]].astype(F32) * w_ref[t, k]
            j = k % n_acc
            acc[j] = term if acc[j] is None else acc[j] + term
        o_ref[t] = (acc[0] + acc[1]) + (acc[2] + acc[3])
        return carry

    lax.fori_loop(0, o_ref.shape[0], token, 0)


def _peer_v(idx_tm, w_tm, tab):
    t, n_sel = idx_tm.shape
    tg = PEER_V_TILE
    smem = lambda: pl.BlockSpec((tg, n_sel), lambda i: (i, 0), memory_space=pltpu.SMEM)
    return pl.pallas_call(
        _peer_v_kernel,
        grid=(t // tg,),
        in_specs=[smem(), smem(),
                  pl.BlockSpec(tab.shape, lambda i: (0, 0, 0), pipeline_mode=pl.Buffered(1))],
        out_specs=pl.BlockSpec((tg,) + tab.shape[1:], lambda i: (i, 0, 0)),
        out_shape=jax.ShapeDtypeStruct((t,) + tab.shape[1:], F32),
        compiler_params=_cparams("arbitrary"),
        name="peer_v",
    )(idx_tm, w_tm, tab)


def _residual_kernel(x_ref, y_ref, mod_ref, o_ref):
    o_ref[...] = x_ref[...] + mod_ref[0, 5:6, :] * y_ref[...]


def _residual(x, y, mods, tile_row, n_tokens):
    d = x.shape[1]
    tm = TOKEN_TILE
    row = lambda i: (i, 0)
    return pl.pallas_call(
        _residual_kernel,
        grid=(n_tokens // tm,),
        in_specs=[pl.BlockSpec((tm, d), row), pl.BlockSpec((tm, d), row),
                  pl.BlockSpec((1,) + mods.shape[1:], lambda i: (tile_row(i), 0, 0))],
        out_specs=pl.BlockSpec((tm, d), row),
        out_shape=jax.ShapeDtypeStruct((n_tokens, d), F32),
        compiler_params=_cparams("arbitrary"),
        name="residual",
    )(x, y, mods)


def _dft_tables(n):
    j = jnp.arange(n, dtype=jnp.int32)
    ang = (2.0 * math.pi / n) * ((j[:, None] * j[None, :]) % n).astype(F32)
    return jnp.cos(ang), jnp.sin(ang)


def _rope_tables(n):
    f = HEAD_DIM // 4
    rows = n // GRID_W
    row = jnp.repeat(jnp.arange(rows), GRID_W).astype(F32)
    col = jnp.tile(jnp.arange(GRID_W), rows).astype(F32)
    inv = ROPE_THETA ** (-jnp.arange(f, dtype=F32) / f)
    ar, ac = row[:, None] * inv, col[:, None] * inv
    cos = jnp.concatenate([jnp.cos(ar), jnp.cos(ar), jnp.cos(ac), jnp.cos(ac)], axis=1)
    sin = jnp.concatenate([-jnp.sin(ar), jnp.sin(ar), -jnp.sin(ac), jnp.sin(ac)], axis=1)
    return cos, sin


def _block_diag(blocks):
    g, r, c = blocks.shape
    out = jnp.zeros((g * r, g * c), blocks.dtype)
    for i in range(g):
        out = out.at[i * r:(i + 1) * r, i * c:(i + 1) * c].set(blocks[i])
    return out


def _seq_dft(fg, n_batch, n_seq, cs):
    w2 = fg.shape[1]
    w = w2 // 2
    g = fg.reshape(n_batch, n_seq, 2, w).transpose(2, 1, 0, 3).reshape(2 * n_seq, n_batch * w)
    tm = min(512, n_seq)
    re = _matmul(cs, g, tm, min(512, n_batch * w))
    return re.reshape(n_seq, n_batch, w).transpose(1, 0, 2).reshape(n_batch * n_seq, w)


def kernel(x, c, ctx, c_ctx, ada_w, ada_b, norm_mix, w_in, fourier_w, pool_w, pool_scale, pool_proj,
           q_norm, k_norm, attn_proj, w_out, norm_ffn, peer_wq, peer_keys, peer_u, peer_v):
    nb, n_lat, d = x.shape
    n_ctx = ctx.shape[1]
    depth = ada_w.shape[0]
    fw = fourier_w.shape[1]
    pw = pool_proj.shape[1]
    qw = attn_proj.shape[1]
    n_heads = qw // HEAD_DIM
    n_kv = n_heads // GQA_GROUP
    kw = n_kv * HEAD_DIM
    widths = (fw, pw, qw, kw)
    t_lat, t_ctx = nb * n_lat, nb * n_ctx
    t_all = t_lat + t_ctx
    n_exp = peer_u.shape[1]
    n_sel = PEER_HEADS * PEER_TOPK
    assert n_lat % TOKEN_TILE == 0 and n_ctx % TOKEN_TILE == 0 and t_all % PEER_U_TILE == 0
    assert d == SUBLANES * LANES

    lat_tiles = t_lat // TOKEN_TILE
    per_sample = n_lat // TOKEN_TILE
    tile_row = lambda i: jnp.where(i < lat_tiles, i // per_sample, nb)

    rows = -(-(nb + 1) // SUBLANES) * SUBLANES
    cvec = jnp.zeros((rows, d), F32).at[:nb].set(c).at[nb].set(c_ctx)
    mods_all = _adaln(cvec, ada_w, ada_b).reshape(depth, rows, 6, d)

    gd = fw // FOURIER_GROUPS
    cc, sc = _dft_tables(gd)
    norm = 1.0 / math.sqrt(gd)
    eye = jnp.eye(FOURIER_GROUPS, dtype=F32)
    dft_c = jnp.concatenate([jnp.kron(eye, cc), jnp.kron(eye, sc)], axis=1) * norm
    dft_c = dft_c.astype(BF16)

    def seq_tables(n):
        cl, sl = _dft_tables(n)
        return (jnp.concatenate([cl, -sl], axis=1) * (1.0 / math.sqrt(n))).astype(BF16)

    cs_lat, cs_ctx = seq_tables(n_lat), seq_tables(n_ctx)
    cos, sin = _rope_tables(n_lat)

    stream = jnp.concatenate([x.reshape(t_lat, d), ctx.reshape(t_ctx, d)], axis=0)

    for l in range(depth):
        last = l == depth - 1
        mods = mods_all[l]
        w_in_b = w_in[l].astype(BF16)
        fg, p, q, k, v, gates = _in_proj(stream, mods, norm_mix[l][None, :], w_in_b, dft_c, tile_row, widths)

        re_lat = _seq_dft(fg[:t_lat], nb, n_lat, cs_lat)
        w_bd = _block_diag(pool_w[l]).astype(BF16)
        scale = pool_scale[l][None, :]
        mx_lat = _pool(p[:t_lat].reshape(nb, n_lat, pw), w_bd, scale).reshape(t_lat, pw)
        def heads(a, n, h):
            return a.reshape(nb, n, h, HEAD_DIM).transpose(0, 2, 1, 3)
        q_lat, q_ctx = heads(q[:t_lat], n_lat, n_heads), heads(q[t_lat:], n_ctx, n_heads)
        k_lat, k_ctx = heads(k[:t_lat], n_lat, n_kv), heads(k[t_lat:], n_ctx, n_kv)
        v_lat, v_ctx = heads(v[:t_lat], n_lat, n_kv), heads(v[t_lat:], n_ctx, n_kv)
        qg, kg = q_norm[l][None, :], k_norm[l][None, :]
        at_lat = _attention(q_lat, jnp.concatenate([k_lat, k_ctx], axis=2),
                            jnp.concatenate([v_lat, v_ctx], axis=2), cos, sin, qg, kg, n_lat)
        at_lat = at_lat.transpose(0, 2, 1, 3).reshape(t_lat, qw)

        if last:
            re, mixed, attn, n_tok = re_lat, mx_lat, at_lat, t_lat
        else:
            re_ctx = _seq_dft(fg[t_lat:], nb, n_ctx, cs_ctx)
            mx_ctx = _pool(p[t_lat:].reshape(nb, n_ctx, pw), w_bd, scale).reshape(t_ctx, pw)
            at_ctx = _attention(q_ctx, k_ctx, v_ctx, cos, sin, qg, kg, 0)
            at_ctx = at_ctx.transpose(0, 2, 1, 3).reshape(t_ctx, qw)
            re = jnp.concatenate([re_lat, re_ctx], axis=0)
            mixed = jnp.concatenate([mx_lat, mx_ctx], axis=0)
            attn = jnp.concatenate([at_lat, at_ctx], axis=0)
            n_tok = t_all

        x1 = _merge(re, mixed, attn, gates, stream, mods,
                    fourier_w[l].astype(BF16), pool_proj[l].astype(BF16),
                    attn_proj[l].astype(BF16), w_out[l].astype(BF16), tile_row, n_tok)

        keys_b = peer_keys[l].reshape(2 * PEER_HEADS, PEER_KEYS, -1).astype(BF16)
        h2, e_t, g_t = _peer_query(x1, mods, norm_ffn[l][None, :], peer_wq[l].astype(BF16), keys_b,
                                   tile_row, n_tok)
        idx_tm = e_t.T
        u_tab = peer_u[l].astype(BF16).reshape(n_exp, SUBLANES, LANES)
        v_tab = peer_v[l].astype(BF16).reshape(n_exp, SUBLANES, LANES)
        w_t = _peer_u(idx_tm, h2.reshape(n_tok, SUBLANES, LANES), u_tab, g_t)
        y = _peer_v(idx_tm, w_t.T, v_tab).reshape(n_tok, d)
        stream = _residual(x1, y, mods, tile_row, n_tok)

    return stream[:t_lat].reshape(nb, n_lat, d)
```

```python
import functools
import math

import jax
import jax.numpy as jnp
from jax import lax
from jax.experimental import pallas as pl
from jax.experimental.pallas import tpu as pltpu

EPS = 1e-6
GRID_W = 64
HEAD_DIM = 64
GQA_GROUP = 4
ROPE_THETA = 10000.0
POOL_WINDOWS = (2, 4, 8, 16)
FOURIER_GROUPS = 4
PEER_HEADS = 8
PEER_KEYS = 128
PEER_TOPK = 16

LANES = 128
SUBLANES = 8
VMEM_LIMIT = 56 * 1024 * 1024

TOKEN_TILE = 256
ATTN_Q_TILE = 256
PEER_U_TILE = 128
PEER_V_TILE = 128

BF16 = jnp.bfloat16
F32 = jnp.float32
NEG_INF = float("-inf")


def _cparams(*sem):
    return pltpu.CompilerParams(dimension_semantics=sem, vmem_limit_bytes=VMEM_LIMIT)


def _rms(x, gain):
    return x * lax.rsqrt(jnp.mean(x * x, axis=-1, keepdims=True) + EPS) * gain


def _resident(shape):
    zeros = (0,) * len(shape)
    return pl.BlockSpec(shape, lambda *_: zeros)


def _adaln_kernel(c_ref, w_ref, b_ref, o_ref):
    c = c_ref[...]
    s = c * jax.nn.sigmoid(c)
    o_ref[0] = jnp.dot(s.astype(BF16), w_ref[0].astype(BF16), preferred_element_type=F32) + b_ref[0]


def _adaln(cvec, ada_w, ada_b):
    depth, d, six_d = ada_w.shape
    rows = cvec.shape[0]
    tn = 1024
    return pl.pallas_call(
        _adaln_kernel,
        grid=(depth, six_d // tn),
        in_specs=[pl.BlockSpec((rows, d), lambda l, j: (0, 0)),
                  pl.BlockSpec((1, d, tn), lambda l, j: (l, 0, j)),
                  pl.BlockSpec((1, 1, tn), lambda l, j: (l, 0, j))],
        out_specs=pl.BlockSpec((1, rows, tn), lambda l, j: (l, 0, j)),
        out_shape=jax.ShapeDtypeStruct((depth, rows, six_d), F32),
        compiler_params=_cparams("arbitrary", "arbitrary"),
        name="adaln",
    )(cvec, ada_w, ada_b.reshape(depth, 1, six_d))


def _proj_kernel(x_ref, mod_ref, g_ref, w_ref, dft_ref,
                 fg_ref, p_ref, q_ref, k_ref, v_ref, gate_ref, *, widths):
    fw, pw, qw, kw = widths
    x = x_ref[...]
    h = _rms(x, g_ref[...]) * (1.0 + mod_ref[0, 1:2, :]) + mod_ref[0, 0:1, :]
    hb = h.astype(BF16)

    def mm(c0, c1):
        return jnp.dot(hb, w_ref[:, c0:c1], preferred_element_type=F32)

    c = 0
    f = mm(c, c + fw); c += fw
    fg_ref[...] = jnp.dot(f.astype(BF16), dft_ref[...], preferred_element_type=F32).astype(BF16)
    p_ref[...] = mm(c, c + pw); c += pw
    q_ref[...] = mm(c, c + qw); c += qw
    k_ref[...] = mm(c, c + kw); c += kw
    v_ref[...] = mm(c, c + kw); c += kw
    n_gate = gate_ref.shape[1]
    step = 1024
    for j in range(0, n_gate, step):
        gate_ref[:, j:j + step] = jax.nn.sigmoid(mm(c + j, c + j + step))


def _in_proj(x, mods, gain, w_in_b, dft_c, tile_row, widths):
    t, d = x.shape
    fw, pw, qw, kw = widths
    n_gate = w_in_b.shape[1] - (fw + pw + qw + 2 * kw)
    tm = TOKEN_TILE
    row = lambda i: (i, 0)
    outs = [(2 * fw, BF16), (pw, F32), (qw, F32), (kw, F32), (kw, F32), (n_gate, F32)]
    return pl.pallas_call(
        functools.partial(_proj_kernel, widths=widths),
        grid=(t // tm,),
        in_specs=[pl.BlockSpec((tm, d), row),
                  pl.BlockSpec((1,) + mods.shape[1:], lambda i: (tile_row(i), 0, 0)),
                  _resident(gain.shape), _resident(w_in_b.shape), _resident(dft_c.shape)],
        out_specs=[pl.BlockSpec((tm, w), row) for w, _ in outs],
        out_shape=[jax.ShapeDtypeStruct((t, w), dt) for w, dt in outs],
        compiler_params=_cparams("arbitrary"),
        name="in_proj",
    )(x, mods, gain, w_in_b, dft_c)


def _mm_kernel(a_ref, b_ref, o_ref):
    o_ref[...] = jnp.dot(a_ref[...], b_ref[...], preferred_element_type=F32)


def _matmul(a, b, tm, tn):
    m, k = a.shape
    n = b.shape[1]
    return pl.pallas_call(
        _mm_kernel,
        grid=(m // tm, n // tn),
        in_specs=[pl.BlockSpec((tm, k), lambda i, j: (i, 0)),
                  pl.BlockSpec((k, tn), lambda i, j: (0, j))],
        out_specs=pl.BlockSpec((tm, tn), lambda i, j: (i, j)),
        out_shape=jax.ShapeDtypeStruct((m, n), F32),
        compiler_params=_cparams("arbitrary", "arbitrary"),
        name="seq_dft",
    )(a, b)


def _pool_kernel(p_ref, w_ref, s_ref, o_ref):
    x = p_ref[0]
    n, width = x.shape
    gdim = width // len(POOL_WINDOWS)
    t = lax.broadcasted_iota(jnp.int32, (n, 1), 0)
    col = lax.broadcasted_iota(jnp.int32, (1, width), 1)
    half = jnp.zeros((1, width), jnp.int32)
    for gi, win in enumerate(POOL_WINDOWS):
        half = jnp.where((col >= gi * gdim) & (col < (gi + 1) * gdim), win // 2, half)
    acc = jnp.zeros_like(x)
    max_half = max(POOL_WINDOWS) // 2
    for d in range(-max_half, max_half):
        shifted = x if d == 0 else pltpu.roll(x, (-d) % n, axis=0)
        row_ok = (t + d >= 0) & (t + d < n)
        col_ok = (half >= -d) if d < 0 else (half > d)
        acc = acc + jnp.where(row_ok, jnp.where(col_ok, shifted, 0.0), 0.0)
    cnt = (jnp.minimum(t + half, n) - jnp.maximum(t - half, 0)).astype(F32)
    pooled = acc / cnt - x
    o_ref[0] = jnp.dot(pooled.astype(BF16), w_ref[...], preferred_element_type=F32) * s_ref[...]


def _pool(p3, w_bd, scale):
    b, n, width = p3.shape
    blk = pl.BlockSpec((1, n, width), lambda i: (i, 0, 0))
    return pl.pallas_call(
        _pool_kernel,
        grid=(b,),
        in_specs=[blk, _resident(w_bd.shape), _resident(scale.shape)],
        out_specs=blk,
        out_shape=jax.ShapeDtypeStruct(p3.shape, F32),
        compiler_params=_cparams("arbitrary"),
        name="pool",
    )(p3, w_bd, scale)


def _swap_rope_halves(x):
    q = HEAD_DIM // 4
    return jnp.concatenate([x[:, q:2 * q], x[:, 0:q], x[:, 3 * q:4 * q], x[:, 2 * q:3 * q]], axis=1)


def _attn_kernel(q_ref, k_ref, v_ref, cos_ref, sin_ref, qn_ref, kn_ref, o_ref, kb_ref, vb_ref,
                 *, n_rope, tq):
    n_q = q_ref.shape[2]
    n_k = k_ref.shape[2]
    kn = _rms(k_ref[0, 0], kn_ref[...])
    if n_rope:
        k_lat = kn[:n_rope]
        kb_ref[0:n_rope, :] = (k_lat * cos_ref[...] + _swap_rope_halves(k_lat) * sin_ref[...]).astype(BF16)
        if n_k > n_rope:
            kb_ref[n_rope:n_k, :] = kn[n_rope:].astype(BF16)
    else:
        kb_ref[...] = kn.astype(BF16)
    vb_ref[...] = v_ref[0, 0].astype(BF16)
    n_tiles = n_q // tq

    def block(i, carry):
        g = i // n_tiles
        r0 = pl.multiple_of((i % n_tiles) * tq, tq)
        qn = _rms(q_ref[0, g, pl.ds(r0, tq), :], qn_ref[...])
        if n_rope:
            qn = qn * cos_ref[pl.ds(r0, tq), :] + _swap_rope_halves(qn) * sin_ref[pl.ds(r0, tq), :]
        qb = (qn * (HEAD_DIM ** -0.5)).astype(BF16)
        s = lax.dot_general(qb, kb_ref[...], (((1,), (1,)), ((), ())), preferred_element_type=F32)
        m = jnp.max(s, axis=-1, keepdims=True)
        p = jnp.exp(s - m)
        l = jnp.sum(p, axis=-1, keepdims=True)
        o = jnp.dot(p.astype(BF16), vb_ref[...], preferred_element_type=F32)
        o_ref[0, g, pl.ds(r0, tq), :] = o / l
        return carry

    lax.fori_loop(0, GQA_GROUP * n_tiles, block, 0)


def _attention(q4, k4, v4, cos, sin, q_gain, k_gain, n_rope):
    b, n_heads, n_q, hd = q4.shape
    n_kv = k4.shape[1]
    n_k = k4.shape[2]
    tq = min(ATTN_Q_TILE, n_q)
    qblk = pl.BlockSpec((1, GQA_GROUP, n_q, hd), lambda i, j: (i, j, 0, 0))
    kblk = pl.BlockSpec((1, 1, n_k, hd), lambda i, j: (i, j, 0, 0))
    return pl.pallas_call(
        functools.partial(_attn_kernel, n_rope=n_rope, tq=tq),
        grid=(b, n_kv),
        in_specs=[qblk, kblk, kblk, _resident(cos.shape), _resident(sin.shape),
                  _resident(q_gain.shape), _resident(k_gain.shape)],
        out_specs=qblk,
        out_shape=jax.ShapeDtypeStruct(q4.shape, F32),
        scratch_shapes=[pltpu.VMEM((n_k, hd), BF16), pltpu.VMEM((n_k, hd), BF16)],
        compiler_params=_cparams("arbitrary", "arbitrary"),
        name="attention",
    )(q4, k4, v4, cos, sin, q_gain, k_gain)


def _merge_kernel(re_ref, mx_ref, at_ref, gate_ref, x_ref, mod_ref, fw_ref, pw_ref, aw_ref, wo_ref, o_ref):
    d = x_ref.shape[1]
    yf = jnp.dot(re_ref[...].astype(BF16), fw_ref[...], preferred_element_type=F32)
    yp = jnp.dot(mx_ref[...].astype(BF16), pw_ref[...], preferred_element_type=F32)
    ya = jnp.dot(at_ref[...].astype(BF16), aw_ref[...], preferred_element_type=F32)
    z = gate_ref[:, 0:d] * yf + gate_ref[:, d:2 * d] * yp + gate_ref[:, 2 * d:3 * d] * ya
    y = jnp.dot(z.astype(BF16), wo_ref[...], preferred_element_type=F32)
    o_ref[...] = x_ref[...] + mod_ref[0, 2:3, :] * y


def _merge(re, mixed, attn, gates, x, mods, fw, pw, aw, wo, tile_row, n_tokens):
    d = x.shape[1]
    tm = TOKEN_TILE
    row = lambda i: (i, 0)
    ins = [re, mixed, attn, gates, x]
    return pl.pallas_call(
        _merge_kernel,
        grid=(n_tokens // tm,),
        in_specs=[pl.BlockSpec((tm, a.shape[1]), row) for a in ins]
        + [pl.BlockSpec((1,) + mods.shape[1:], lambda i: (tile_row(i), 0, 0))]
        + [_resident(w.shape) for w in (fw, pw, aw, wo)],
        out_specs=pl.BlockSpec((tm, d), row),
        out_shape=jax.ShapeDtypeStruct((n_tokens, d), F32),
        compiler_params=_cparams("arbitrary"),
        name="merge",
    )(*ins, mods, fw, pw, aw, wo)


def _topk_rows(s, k):
    n = s.shape[0]
    rows = lax.broadcasted_iota(jnp.int32, s.shape, 0)
    vals, ids = [], []
    for _ in range(k):
        m = jnp.max(s, axis=0, keepdims=True)
        r = jnp.min(jnp.where(s == m, rows, n), axis=0, keepdims=True)
        vals.append(m)
        ids.append(r)
        s = jnp.where(rows == r, NEG_INF, s)
    return jnp.concatenate(vals, axis=0), jnp.concatenate(ids, axis=0)


def _pair_candidates(s1, i1, s2, i2):
    k = PEER_TOPK
    row8 = lax.broadcasted_iota(jnp.int32, (SUBLANES, 1), 0)
    cand, expert = [s1[0:1] + s2], [i1[0:1] * PEER_KEYS + i2]
    for a in range(1, SUBLANES):
        nb = k // (a + 1)
        c = s1[a:a + 1] + s2[0:SUBLANES]
        cand.append(jnp.where(row8 < nb, c, NEG_INF))
        expert.append(i1[a:a + 1] * PEER_KEYS + i2[0:SUBLANES])
    cand.append(s1[SUBLANES:k] + s2[0:1])
    expert.append(i1[SUBLANES:k] * PEER_KEYS + i2[0:1])
    return jnp.concatenate(cand, axis=0), jnp.concatenate(expert, axis=0)


def _peer_q_kernel(x_ref, mod_ref, g_ref, wq_ref, keys_ref, h_ref, e_ref, gw_ref, q_scr):
    kd = keys_ref.shape[2]
    h = _rms(x_ref[...], g_ref[...]) * (1.0 + mod_ref[0, 4:5, :]) + mod_ref[0, 3:4, :]
    h_ref[...] = h
    hb = h.astype(BF16)
    for j in range(2 * PEER_HEADS):
        q_scr[j] = jnp.dot(hb, wq_ref[:, j * kd:(j + 1) * kd], preferred_element_type=F32).astype(BF16)

    def head(hh, carry):
        tops = []
        for p in range(2):
            s = lax.dot_general(keys_ref[2 * hh + p], q_scr[2 * hh + p], (((1,), (1,)), ((), ())),
                                preferred_element_type=F32)
            tops.append(_topk_rows(s, PEER_TOPK))
        cand, expert = _pair_candidates(tops[0][0], tops[0][1], tops[1][0], tops[1][1])
        n = cand.shape[0]
        rows = lax.broadcasted_iota(jnp.int32, cand.shape, 0)
        top, eid = [], []
        for _ in range(PEER_TOPK):
            m = jnp.max(cand, axis=0, keepdims=True)
            r = jnp.min(jnp.where(cand == m, rows, n), axis=0, keepdims=True)
            sel = rows == r
            top.append(m)
            eid.append(jnp.max(jnp.where(sel, expert, -1), axis=0, keepdims=True))
            cand = jnp.where(sel, NEG_INF, cand)
        top = jnp.concatenate(top, axis=0)
        ex = jnp.exp(top - top[0:1])
        r0 = pl.multiple_of(hh * PEER_TOPK, PEER_TOPK)
        gw_ref[pl.ds(r0, PEER_TOPK), :] = ex / jnp.sum(ex, axis=0, keepdims=True)
        e_ref[pl.ds(r0, PEER_TOPK), :] = jnp.concatenate(eid, axis=0)
        return carry

    lax.fori_loop(0, PEER_HEADS, head, 0)


def _peer_query(x, mods, gain, wq_b, keys_b, tile_row, n_tokens):
    d = x.shape[1]
    tm = TOKEN_TILE
    n_sel = PEER_HEADS * PEER_TOPK
    kd = keys_b.shape[2]
    return pl.pallas_call(
        _peer_q_kernel,
        grid=(n_tokens // tm,),
        in_specs=[pl.BlockSpec((tm, d), lambda i: (i, 0)),
                  pl.BlockSpec((1,) + mods.shape[1:], lambda i: (tile_row(i), 0, 0)),
                  _resident(gain.shape), _resident(wq_b.shape), _resident(keys_b.shape)],
        out_specs=[pl.BlockSpec((tm, d), lambda i: (i, 0)),
                   pl.BlockSpec((n_sel, tm), lambda i: (0, i)),
                   pl.BlockSpec((n_sel, tm), lambda i: (0, i))],
        out_shape=[jax.ShapeDtypeStruct((n_tokens, d), F32),
                   jax.ShapeDtypeStruct((n_sel, n_tokens), jnp.int32),
                   jax.ShapeDtypeStruct((n_sel, n_tokens), F32)],
        scratch_shapes=[pltpu.VMEM((2 * PEER_HEADS, tm, kd), BF16)],
        compiler_params=_cparams("arbitrary"),
        name="peer_query",
    )(x, mods, gain, wq_b, keys_b)


def _fold_pairs(vs, shift):
    sub = lax.broadcasted_iota(jnp.int32, (SUBLANES, LANES), 0)
    keep = (sub % (2 * shift)) < shift
    out = []
    for i in range(0, len(vs), 2):
        a = vs[i] + pltpu.roll(vs[i], SUBLANES - shift, axis=0)
        b = vs[i + 1] + pltpu.roll(vs[i + 1], SUBLANES - shift, axis=0)
        out.append(jnp.where(keep, a, pltpu.roll(b, shift, axis=0)))
    return out


def _trace_fold_rows():
    vs = [[i] * SUBLANES for i in range(SUBLANES)]
    for shift in (4, 2, 1):
        vs = [[vs[i][r] if (r % (2 * shift)) < shift else vs[i + 1][r - shift] for r in range(SUBLANES)]
              for i in range(0, len(vs), 2)]
    return vs[0]


_SLOT_PERM = _trace_fold_rows()


def _pack_rows(tab):
    n = tab.shape[0]
    t = tab.astype(BF16).reshape(n, SUBLANES // 2, 2, LANES).transpose(0, 1, 3, 2)
    return lax.bitcast_convert_type(t, jnp.uint32).reshape(n * (SUBLANES // 2), LANES)


def _gather_row(tab_ref, row4):
    words = tab_ref[pl.ds(pl.multiple_of(row4, SUBLANES // 2), SUBLANES // 2), :]
    return pltpu.bitcast(words, BF16).astype(F32)


def _peer_u_kernel(idx_ref, x_ref, tab_ref, gw_ref, o_ref, a_scr):
    n_sel = a_scr.shape[0]
    lane = lax.broadcasted_iota(jnp.int32, (n_sel, LANES), 1)
    a_scr[...] = jnp.zeros_like(a_scr)

    def token(t, carry):
        xt = x_ref[t]
        row = idx_ref.at[pl.ds(t * n_sel, n_sel)]
        prods = [None] * n_sel
        for k in range(n_sel):
            f, s = divmod(k, SUBLANES)
            prods[f * SUBLANES + _SLOT_PERM[s]] = _gather_row(tab_ref, row[k]) * xt
        for shift in (4, 2, 1):
            prods = _fold_pairs(prods, shift)
        m = jnp.concatenate(prods, axis=0)
        col = jnp.sum(m, axis=1, keepdims=True)
        a_scr[...] = jnp.where(lane == t, col, a_scr[...])
        return carry

    lax.fori_loop(0, x_ref.shape[0], token, 0)
    o_ref[...] = gw_ref[...] * jax.nn.gelu(a_scr[...])


def _peer_u(idx_flat, h3, tab, gw):
    t = h3.shape[0]
    n_sel = gw.shape[0]
    tg = PEER_U_TILE
    return pl.pallas_call(
        _peer_u_kernel,
        grid=(t // tg,),
        in_specs=[pl.BlockSpec((tg * n_sel,), lambda i: (i,), memory_space=pltpu.SMEM),
                  pl.BlockSpec((tg,) + h3.shape[1:], lambda i: (i, 0, 0)),
                  pl.BlockSpec(tab.shape, lambda i: (0, 0), pipeline_mode=pl.Buffered(1)),
                  pl.BlockSpec((n_sel, tg), lambda i: (0, i))],
        out_specs=pl.BlockSpec((n_sel, tg), lambda i: (0, i)),
        out_shape=jax.ShapeDtypeStruct((n_sel, t), F32),
        scratch_shapes=[pltpu.VMEM((n_sel, LANES), F32)],
        compiler_params=_cparams("arbitrary"),
        name="peer_u",
    )(idx_flat, h3, tab, gw)


def _peer_v_kernel(idx_ref, w_ref, tab_ref, o_ref, wb_scr):
    n_sel, tg = w_ref.shape
    n_acc = 4
    for t in range(tg):
        wb_scr[t] = jnp.broadcast_to(w_ref[:, t:t + 1], (n_sel, LANES))

    def token(t, carry):
        row = idx_ref.at[pl.ds(t * n_sel, n_sel)]
        acc = [None] * n_acc
        for k in range(n_sel):
            term = _gather_row(tab_ref, row[k]) * wb_scr[t, k:k + 1, :]
            j = k % n_acc
            acc[j] = term if acc[j] is None else acc[j] + term
        o_ref[t] = (acc[0] + acc[1]) + (acc[2] + acc[3])
        return carry

    lax.fori_loop(0, tg, token, 0)


def _peer_v(idx_flat, w_km, tab):
    n_sel, t = w_km.shape
    tg = PEER_V_TILE
    return pl.pallas_call(
        _peer_v_kernel,
        grid=(t // tg,),
        in_specs=[pl.BlockSpec((tg * n_sel,), lambda i: (i,), memory_space=pltpu.SMEM),
                  pl.BlockSpec((n_sel, tg), lambda i: (0, i)),
                  pl.BlockSpec(tab.shape, lambda i: (0, 0), pipeline_mode=pl.Buffered(1))],
        out_specs=pl.BlockSpec((tg, SUBLANES, LANES), lambda i: (i, 0, 0)),
        out_shape=jax.ShapeDtypeStruct((t, SUBLANES, LANES), F32),
        scratch_shapes=[pltpu.VMEM((tg, n_sel, LANES), F32)],
        compiler_params=_cparams("arbitrary"),
        name="peer_v",
    )(idx_flat, w_km, tab)


def _residual_kernel(x_ref, y_ref, mod_ref, o_ref):
    o_ref[...] = x_ref[...] + mod_ref[0, 5:6, :] * y_ref[...]


def _residual(x, y, mods, tile_row, n_tokens):
    d = x.shape[1]
    tm = TOKEN_TILE
    row = lambda i: (i, 0)
    return pl.pallas_call(
        _residual_kernel,
        grid=(n_tokens // tm,),
        in_specs=[pl.BlockSpec((tm, d), row), pl.BlockSpec((tm, d), row),
                  pl.BlockSpec((1,) + mods.shape[1:], lambda i: (tile_row(i), 0, 0))],
        out_specs=pl.BlockSpec((tm, d), row),
        out_shape=jax.ShapeDtypeStruct((n_tokens, d), F32),
        compiler_params=_cparams("arbitrary"),
        name="residual",
    )(x, y, mods)


def _dft_tables(n):
    j = jnp.arange(n, dtype=jnp.int32)
    ang = (2.0 * math.pi / n) * ((j[:, None] * j[None, :]) % n).astype(F32)
    return jnp.cos(ang), jnp.sin(ang)


def _rope_tables(n):
    f = HEAD_DIM // 4
    rows = n // GRID_W
    row = jnp.repeat(jnp.arange(rows), GRID_W).astype(F32)
    col = jnp.tile(jnp.arange(GRID_W), rows).astype(F32)
    inv = ROPE_THETA ** (-jnp.arange(f, dtype=F32) / f)
    ar, ac = row[:, None] * inv, col[:, None] * inv
    cos = jnp.concatenate([jnp.cos(ar), jnp.cos(ar), jnp.cos(ac), jnp.cos(ac)], axis=1)
    sin = jnp.concatenate([-jnp.sin(ar), jnp.sin(ar), -jnp.sin(ac), jnp.sin(ac)], axis=1)
    return cos, sin


def _block_diag(blocks):
    g, r, c = blocks.shape
    out = jnp.zeros((g * r, g * c), blocks.dtype)
    for i in range(g):
        out = out.at[i * r:(i + 1) * r, i * c:(i + 1) * c].set(blocks[i])
    return out


def _seq_dft(fg, n_batch, n_seq, cs):
    w2 = fg.shape[1]
    w = w2 // 2
    g = fg.reshape(n_batch, n_seq, 2, w).transpose(2, 1, 0, 3).reshape(2 * n_seq, n_batch * w)
    tm = min(512, n_seq)
    re = _matmul(cs, g, tm, min(512, n_batch * w))
    return re.reshape(n_seq, n_batch, w).transpose(1, 0, 2).reshape(n_batch * n_seq, w)


def kernel(x, c, ctx, c_ctx, ada_w, ada_b, norm_mix, w_in, fourier_w, pool_w, pool_scale, pool_proj,
           q_norm, k_norm, attn_proj, w_out, norm_ffn, peer_wq, peer_keys, peer_u, peer_v):
    nb, n_lat, d = x.shape
    n_ctx = ctx.shape[1]
    depth = ada_w.shape[0]
    fw = fourier_w.shape[1]
    pw = pool_proj.shape[1]
    qw = attn_proj.shape[1]
    n_heads = qw // HEAD_DIM
    n_kv = n_heads // GQA_GROUP
    kw = n_kv * HEAD_DIM
    widths = (fw, pw, qw, kw)
    t_lat, t_ctx = nb * n_lat, nb * n_ctx
    t_all = t_lat + t_ctx
    n_exp = peer_u.shape[1]
    n_sel = PEER_HEADS * PEER_TOPK
    assert n_lat % TOKEN_TILE == 0 and n_ctx % TOKEN_TILE == 0 and t_all % PEER_U_TILE == 0
    assert d == SUBLANES * LANES

    lat_tiles = t_lat // TOKEN_TILE
    per_sample = n_lat // TOKEN_TILE
    tile_row = lambda i: jnp.where(i < lat_tiles, i // per_sample, nb)

    rows = -(-(nb + 1) // SUBLANES) * SUBLANES
    cvec = jnp.zeros((rows, d), F32).at[:nb].set(c).at[nb].set(c_ctx)
    mods_all = _adaln(cvec, ada_w, ada_b).reshape(depth, rows, 6, d)

    gd = fw // FOURIER_GROUPS
    cc, sc = _dft_tables(gd)
    norm = 1.0 / math.sqrt(gd)
    eye = jnp.eye(FOURIER_GROUPS, dtype=F32)
    dft_c = jnp.concatenate([jnp.kron(eye, cc), jnp.kron(eye, sc)], axis=1) * norm
    dft_c = dft_c.astype(BF16)

    def seq_tables(n):
        cl, sl = _dft_tables(n)
        return (jnp.concatenate([cl, -sl], axis=1) * (1.0 / math.sqrt(n))).astype(BF16)

    cs_lat, cs_ctx = seq_tables(n_lat), seq_tables(n_ctx)
    cos, sin = _rope_tables(n_lat)

    stream = jnp.concatenate([x.reshape(t_lat, d), ctx.reshape(t_ctx, d)], axis=0)

    for l in range(depth):
        last = l == depth - 1
        mods = mods_all[l]
        w_in_b = w_in[l].astype(BF16)
        fg, p, q, k, v, gates = _in_proj(stream, mods, norm_mix[l][None, :], w_in_b, dft_c, tile_row, widths)

        re_lat = _seq_dft(fg[:t_lat], nb, n_lat, cs_lat)
        w_bd = _block_diag(pool_w[l]).astype(BF16)
        scale = pool_scale[l][None, :]
        mx_lat = _pool(p[:t_lat].reshape(nb, n_lat, pw), w_bd, scale).reshape(t_lat, pw)
        def heads(a, n, h):
            return a.reshape(nb, n, h, HEAD_DIM).transpose(0, 2, 1, 3)
        q_lat, q_ctx = heads(q[:t_lat], n_lat, n_heads), heads(q[t_lat:], n_ctx, n_heads)
        k_lat, k_ctx = heads(k[:t_lat], n_lat, n_kv), heads(k[t_lat:], n_ctx, n_kv)
        v_lat, v_ctx = heads(v[:t_lat], n_lat, n_kv), heads(v[t_lat:], n_ctx, n_kv)
        qg, kg = q_norm[l][None, :], k_norm[l][None, :]
        at_lat = _attention(q_lat, jnp.concatenate([k_lat, k_ctx], axis=2),
                            jnp.concatenate([v_lat, v_ctx], axis=2), cos, sin, qg, kg, n_lat)
        at_lat = at_lat.transpose(0, 2, 1, 3).reshape(t_lat, qw)

        if last:
            re, mixed, attn, n_tok = re_lat, mx_lat, at_lat, t_lat
        else:
            re_ctx = _seq_dft(fg[t_lat:], nb, n_ctx, cs_ctx)
            mx_ctx = _pool(p[t_lat:].reshape(nb, n_ctx, pw), w_bd, scale).reshape(t_ctx, pw)
            at_ctx = _attention(q_ctx, k_ctx, v_ctx, cos, sin, qg, kg, 0)
            at_ctx = at_ctx.transpose(0, 2, 1, 3).reshape(t_ctx, qw)
            re = jnp.concatenate([re_lat, re_ctx], axis=0)
            mixed = jnp.concatenate([mx_lat, mx_ctx], axis=0)
            attn = jnp.concatenate([at_lat, at_ctx], axis=0)
            n_tok = t_all

        x1 = _merge(re, mixed, attn, gates, stream, mods,
                    fourier_w[l].astype(BF16), pool_proj[l].astype(BF16),
                    attn_proj[l].astype(BF16), w_out[l].astype(BF16), tile_row, n_tok)

        keys_b = peer_keys[l].reshape(2 * PEER_HEADS, PEER_KEYS, -1).astype(BF16)
        h2, e_t, g_t = _peer_query(x1, mods, norm_ffn[l][None, :], peer_wq[l].astype(BF16), keys_b,
                                   tile_row, n_tok)
        idx4 = (e_t.T * (SUBLANES // 2)).reshape(-1)
        w_t = _peer_u(idx4, h2.reshape(n_tok, SUBLANES, LANES), _pack_rows(peer_u[l]), g_t)
        y = _peer_v(idx4, w_t, _pack_rows(peer_v[l])).reshape(n_tok, d)
        stream = _residual(x1, y, mods, tile_row, n_tok)

    return stream[:t_lat].reshape(nb, n_lat, d)
```

```python
import functools
import math

import jax
import jax.numpy as jnp
from jax import lax
from jax.experimental import pallas as pl
from jax.experimental.pallas import tpu as pltpu

EPS = 1e-6
GRID_W = 64
HEAD_DIM = 64
GQA_GROUP = 4
ROPE_THETA = 10000.0
POOL_WINDOWS = (2, 4, 8, 16)
FOURIER_GROUPS = 4
PEER_HEADS = 8
PEER_KEYS = 128
PEER_TOPK = 16

LANES = 128
SUBLANES = 8
VMEM_LIMIT = 56 * 1024 * 1024

TOKEN_TILE = 256
ATTN_Q_TILE = 256
PEER_U_TILE = 128
PEER_V_TILE = 128

BF16 = jnp.bfloat16
F32 = jnp.float32
NEG_INF = float("-inf")


def _cparams(*sem):
    return pltpu.CompilerParams(dimension_semantics=sem, vmem_limit_bytes=VMEM_LIMIT)


def _rms(x, gain):
    return x * lax.rsqrt(jnp.mean(x * x, axis=-1, keepdims=True) + EPS) * gain


def _resident(shape):
    zeros = (0,) * len(shape)
    return pl.BlockSpec(shape, lambda *_: zeros)


def _adaln_kernel(c_ref, w_ref, b_ref, o_ref):
    c = c_ref[...]
    s = c * jax.nn.sigmoid(c)
    o_ref[0] = jnp.dot(s.astype(BF16), w_ref[0].astype(BF16), preferred_element_type=F32) + b_ref[0]


def _adaln(cvec, ada_w, ada_b):
    depth, d, six_d = ada_w.shape
    rows = cvec.shape[0]
    tn = 1024
    return pl.pallas_call(
        _adaln_kernel,
        grid=(depth, six_d // tn),
        in_specs=[pl.BlockSpec((rows, d), lambda l, j: (0, 0)),
                  pl.BlockSpec((1, d, tn), lambda l, j: (l, 0, j)),
                  pl.BlockSpec((1, 1, tn), lambda l, j: (l, 0, j))],
        out_specs=pl.BlockSpec((1, rows, tn), lambda l, j: (l, 0, j)),
        out_shape=jax.ShapeDtypeStruct((depth, rows, six_d), F32),
        compiler_params=_cparams("arbitrary", "arbitrary"),
        name="adaln",
    )(cvec, ada_w, ada_b.reshape(depth, 1, six_d))


def _proj_kernel(x_ref, mod_ref, g_ref, w_ref, dft_ref,
                 fg_ref, p_ref, q_ref, k_ref, v_ref, gate_ref, *, widths):
    fw, pw, qw, kw = widths
    x = x_ref[...]
    h = _rms(x, g_ref[...]) * (1.0 + mod_ref[0, 1:2, :]) + mod_ref[0, 0:1, :]
    hb = h.astype(BF16)

    def mm(c0, c1):
        return jnp.dot(hb, w_ref[:, c0:c1], preferred_element_type=F32)

    c = 0
    f = mm(c, c + fw); c += fw
    fg_ref[...] = jnp.dot(f.astype(BF16), dft_ref[...], preferred_element_type=F32).astype(BF16)
    p_ref[...] = mm(c, c + pw); c += pw
    q_ref[...] = mm(c, c + qw); c += qw
    k_ref[...] = mm(c, c + kw); c += kw
    v_ref[...] = mm(c, c + kw); c += kw
    n_gate = gate_ref.shape[1]
    step = 1024
    for j in range(0, n_gate, step):
        gate_ref[:, j:j + step] = jax.nn.sigmoid(mm(c + j, c + j + step))


def _in_proj(x, mods, gain, w_in_b, dft_c, tile_row, widths):
    t, d = x.shape
    fw, pw, qw, kw = widths
    n_gate = w_in_b.shape[1] - (fw + pw + qw + 2 * kw)
    tm = TOKEN_TILE
    row = lambda i: (i, 0)
    outs = [(2 * fw, BF16), (pw, F32), (qw, F32), (kw, F32), (kw, F32), (n_gate, F32)]
    return pl.pallas_call(
        functools.partial(_proj_kernel, widths=widths),
        grid=(t // tm,),
        in_specs=[pl.BlockSpec((tm, d), row),
                  pl.BlockSpec((1,) + mods.shape[1:], lambda i: (tile_row(i), 0, 0)),
                  _resident(gain.shape), _resident(w_in_b.shape), _resident(dft_c.shape)],
        out_specs=[pl.BlockSpec((tm, w), row) for w, _ in outs],
        out_shape=[jax.ShapeDtypeStruct((t, w), dt) for w, dt in outs],
        compiler_params=_cparams("arbitrary"),
        name="in_proj",
    )(x, mods, gain, w_in_b, dft_c)


def _mm_kernel(a_ref, b_ref, o_ref):
    o_ref[...] = jnp.dot(a_ref[...], b_ref[...], preferred_element_type=F32)


def _matmul(a, b, tm, tn):
    m, k = a.shape
    n = b.shape[1]
    return pl.pallas_call(
        _mm_kernel,
        grid=(m // tm, n // tn),
        in_specs=[pl.BlockSpec((tm, k), lambda i, j: (i, 0)),
                  pl.BlockSpec((k, tn), lambda i, j: (0, j))],
        out_specs=pl.BlockSpec((tm, tn), lambda i, j: (i, j)),
        out_shape=jax.ShapeDtypeStruct((m, n), F32),
        compiler_params=_cparams("arbitrary", "arbitrary"),
        name="seq_dft",
    )(a, b)


def _pool_kernel(p_ref, w_ref, s_ref, o_ref):
    x = p_ref[0]
    n, width = x.shape
    gdim = width // len(POOL_WINDOWS)
    t = lax.broadcasted_iota(jnp.int32, (n, 1), 0)
    col = lax.broadcasted_iota(jnp.int32, (1, width), 1)
    half = jnp.zeros((1, width), jnp.int32)
    for gi, win in enumerate(POOL_WINDOWS):
        half = jnp.where((col >= gi * gdim) & (col < (gi + 1) * gdim), win // 2, half)
    acc = jnp.zeros_like(x)
    max_half = max(POOL_WINDOWS) // 2
    for d in range(-max_half, max_half):
        shifted = x if d == 0 else pltpu.roll(x, (-d) % n, axis=0)
        row_ok = (t + d >= 0) & (t + d < n)
        col_ok = (half >= -d) if d < 0 else (half > d)
        acc = acc + jnp.where(row_ok, jnp.where(col_ok, shifted, 0.0), 0.0)
    cnt = (jnp.minimum(t + half, n) - jnp.maximum(t - half, 0)).astype(F32)
    pooled = acc / cnt - x
    o_ref[0] = jnp.dot(pooled.astype(BF16), w_ref[...], preferred_element_type=F32) * s_ref[...]


def _pool(p3, w_bd, scale):
    b, n, width = p3.shape
    blk = pl.BlockSpec((1, n, width), lambda i: (i, 0, 0))
    return pl.pallas_call(
        _pool_kernel,
        grid=(b,),
        in_specs=[blk, _resident(w_bd.shape), _resident(scale.shape)],
        out_specs=blk,
        out_shape=jax.ShapeDtypeStruct(p3.shape, F32),
        compiler_params=_cparams("arbitrary"),
        name="pool",
    )(p3, w_bd, scale)


def _swap_rope_halves(x):
    q = HEAD_DIM // 4
    return jnp.concatenate([x[:, q:2 * q], x[:, 0:q], x[:, 3 * q:4 * q], x[:, 2 * q:3 * q]], axis=1)


def _attn_kernel(q_ref, k_ref, v_ref, cos_ref, sin_ref, qn_ref, kn_ref, o_ref, kb_ref, vb_ref,
                 *, n_rope, tq):
    n_q = q_ref.shape[2]
    n_k = k_ref.shape[2]
    kn = _rms(k_ref[0, 0], kn_ref[...])
    if n_rope:
        k_lat = kn[:n_rope]
        kb_ref[0:n_rope, :] = (k_lat * cos_ref[...] + _swap_rope_halves(k_lat) * sin_ref[...]).astype(BF16)
        if n_k > n_rope:
            kb_ref[n_rope:n_k, :] = kn[n_rope:].astype(BF16)
    else:
        kb_ref[...] = kn.astype(BF16)
    vb_ref[...] = v_ref[0, 0].astype(BF16)
    n_tiles = n_q // tq

    def block(i, carry):
        g = i // n_tiles
        r0 = pl.multiple_of((i % n_tiles) * tq, tq)
        qn = _rms(q_ref[0, g, pl.ds(r0, tq), :], qn_ref[...])
        if n_rope:
            qn = qn * cos_ref[pl.ds(r0, tq), :] + _swap_rope_halves(qn) * sin_ref[pl.ds(r0, tq), :]
        qb = (qn * (HEAD_DIM ** -0.5)).astype(BF16)
        s = lax.dot_general(qb, kb_ref[...], (((1,), (1,)), ((), ())), preferred_element_type=F32)
        m = jnp.max(s, axis=-1, keepdims=True)
        p = jnp.exp(s - m)
        l = jnp.sum(p, axis=-1, keepdims=True)
        o = jnp.dot(p.astype(BF16), vb_ref[...], preferred_element_type=F32)
        o_ref[0, g, pl.ds(r0, tq), :] = o / l
        return carry

    lax.fori_loop(0, GQA_GROUP * n_tiles, block, 0)


def _attention(q4, k4, v4, cos, sin, q_gain, k_gain, n_rope):
    b, n_heads, n_q, hd = q4.shape
    n_kv = k4.shape[1]
    n_k = k4.shape[2]
    tq = min(ATTN_Q_TILE, n_q)
    qblk = pl.BlockSpec((1, GQA_GROUP, n_q, hd), lambda i, j: (i, j, 0, 0))
    kblk = pl.BlockSpec((1, 1, n_k, hd), lambda i, j: (i, j, 0, 0))
    return pl.pallas_call(
        functools.partial(_attn_kernel, n_rope=n_rope, tq=tq),
        grid=(b, n_kv),
        in_specs=[qblk, kblk, kblk, _resident(cos.shape), _resident(sin.shape),
                  _resident(q_gain.shape), _resident(k_gain.shape)],
        out_specs=qblk,
        out_shape=jax.ShapeDtypeStruct(q4.shape, F32),
        scratch_shapes=[pltpu.VMEM((n_k, hd), BF16), pltpu.VMEM((n_k, hd), BF16)],
        compiler_params=_cparams("arbitrary", "arbitrary"),
        name="attention",
    )(q4, k4, v4, cos, sin, q_gain, k_gain)


def _merge_kernel(re_ref, mx_ref, at_ref, gate_ref, x_ref, mod_ref, fw_ref, pw_ref, aw_ref, wo_ref, o_ref):
    d = x_ref.shape[1]
    yf = jnp.dot(re_ref[...].astype(BF16), fw_ref[...], preferred_element_type=F32)
    yp = jnp.dot(mx_ref[...].astype(BF16), pw_ref[...], preferred_element_type=F32)
    ya = jnp.dot(at_ref[...].astype(BF16), aw_ref[...], preferred_element_type=F32)
    z = gate_ref[:, 0:d] * yf + gate_ref[:, d:2 * d] * yp + gate_ref[:, 2 * d:3 * d] * ya
    y = jnp.dot(z.astype(BF16), wo_ref[...], preferred_element_type=F32)
    o_ref[...] = x_ref[...] + mod_ref[0, 2:3, :] * y


def _merge(re, mixed, attn, gates, x, mods, fw, pw, aw, wo, tile_row, n_tokens):
    d = x.shape[1]
    tm = TOKEN_TILE
    row = lambda i: (i, 0)
    ins = [re, mixed, attn, gates, x]
    return pl.pallas_call(
        _merge_kernel,
        grid=(n_tokens // tm,),
        in_specs=[pl.BlockSpec((tm, a.shape[1]), row) for a in ins]
        + [pl.BlockSpec((1,) + mods.shape[1:], lambda i: (tile_row(i), 0, 0))]
        + [_resident(w.shape) for w in (fw, pw, aw, wo)],
        out_specs=pl.BlockSpec((tm, d), row),
        out_shape=jax.ShapeDtypeStruct((n_tokens, d), F32),
        compiler_params=_cparams("arbitrary"),
        name="merge",
    )(*ins, mods, fw, pw, aw, wo)


def _topk_rows(s, k):
    n = s.shape[0]
    rows = lax.broadcasted_iota(jnp.int32, s.shape, 0)
    vals, ids = [], []
    for _ in range(k):
        m = jnp.max(s, axis=0, keepdims=True)
        r = jnp.min(jnp.where(s == m, rows, n), axis=0, keepdims=True)
        vals.append(m)
        ids.append(r)
        s = jnp.where(rows == r, NEG_INF, s)
    return jnp.concatenate(vals, axis=0), jnp.concatenate(ids, axis=0)


def _pair_candidates(s1, i1, s2, i2):
    k = PEER_TOPK
    row8 = lax.broadcasted_iota(jnp.int32, (SUBLANES, 1), 0)
    cand, expert = [s1[0:1] + s2], [i1[0:1] * PEER_KEYS + i2]
    for a in range(1, SUBLANES):
        nb = k // (a + 1)
        c = s1[a:a + 1] + s2[0:SUBLANES]
        cand.append(jnp.where(row8 < nb, c, NEG_INF))
        expert.append(i1[a:a + 1] * PEER_KEYS + i2[0:SUBLANES])
    cand.append(s1[SUBLANES:k] + s2[0:1])
    expert.append(i1[SUBLANES:k] * PEER_KEYS + i2[0:1])
    return jnp.concatenate(cand, axis=0), jnp.concatenate(expert, axis=0)


def _peer_q_kernel(x_ref, mod_ref, g_ref, wq_ref, keys_ref, h_ref, e_ref, gw_ref, q_scr):
    kd = keys_ref.shape[2]
    h = _rms(x_ref[...], g_ref[...]) * (1.0 + mod_ref[0, 4:5, :]) + mod_ref[0, 3:4, :]
    h_ref[...] = h
    hb = h.astype(BF16)
    for j in range(2 * PEER_HEADS):
        q_scr[j] = jnp.dot(hb, wq_ref[:, j * kd:(j + 1) * kd], preferred_element_type=F32).astype(BF16)

    def head(hh, carry):
        tops = []
        for p in range(2):
            s = lax.dot_general(keys_ref[2 * hh + p], q_scr[2 * hh + p], (((1,), (1,)), ((), ())),
                                preferred_element_type=F32)
            tops.append(_topk_rows(s, PEER_TOPK))
        cand, expert = _pair_candidates(tops[0][0], tops[0][1], tops[1][0], tops[1][1])
        n = cand.shape[0]
        rows = lax.broadcasted_iota(jnp.int32, cand.shape, 0)
        top, eid = [], []
        for _ in range(PEER_TOPK):
            m = jnp.max(cand, axis=0, keepdims=True)
            r = jnp.min(jnp.where(cand == m, rows, n), axis=0, keepdims=True)
            sel = rows == r
            top.append(m)
            eid.append(jnp.max(jnp.where(sel, expert, -1), axis=0, keepdims=True))
            cand = jnp.where(sel, NEG_INF, cand)
        top = jnp.concatenate(top, axis=0)
        ex = jnp.exp(top - top[0:1])
        r0 = pl.multiple_of(hh * PEER_TOPK, PEER_TOPK)
        gw_ref[pl.ds(r0, PEER_TOPK), :] = ex / jnp.sum(ex, axis=0, keepdims=True)
        e_ref[pl.ds(r0, PEER_TOPK), :] = jnp.concatenate(eid, axis=0)
        return carry

    lax.fori_loop(0, PEER_HEADS, head, 0)


def _peer_query(x, mods, gain, wq_b, keys_b, tile_row, n_tokens):
    d = x.shape[1]
    tm = TOKEN_TILE
    n_sel = PEER_HEADS * PEER_TOPK
    kd = keys_b.shape[2]
    return pl.pallas_call(
        _peer_q_kernel,
        grid=(n_tokens // tm,),
        in_specs=[pl.BlockSpec((tm, d), lambda i: (i, 0)),
                  pl.BlockSpec((1,) + mods.shape[1:], lambda i: (tile_row(i), 0, 0)),
                  _resident(gain.shape), _resident(wq_b.shape), _resident(keys_b.shape)],
        out_specs=[pl.BlockSpec((tm, d), lambda i: (i, 0)),
                   pl.BlockSpec((n_sel, tm), lambda i: (0, i)),
                   pl.BlockSpec((n_sel, tm), lambda i: (0, i))],
        out_shape=[jax.ShapeDtypeStruct((n_tokens, d), F32),
                   jax.ShapeDtypeStruct((n_sel, n_tokens), jnp.int32),
                   jax.ShapeDtypeStruct((n_sel, n_tokens), F32)],
        scratch_shapes=[pltpu.VMEM((2 * PEER_HEADS, tm, kd), BF16)],
        compiler_params=_cparams("arbitrary"),
        name="peer_query",
    )(x, mods, gain, wq_b, keys_b)


def _fold_pairs(vs, shift):
    sub = lax.broadcasted_iota(jnp.int32, vs[0].shape, 0)
    keep = (sub % (2 * shift)) < shift
    out = []
    for i in range(0, len(vs), 2):
        a = vs[i] + pltpu.roll(vs[i], SUBLANES - shift, axis=0)
        b = vs[i + 1] + pltpu.roll(vs[i + 1], SUBLANES - shift, axis=0)
        out.append(jnp.where(keep, a, pltpu.roll(b, shift, axis=0)))
    return out


def _trace_fold_rows():
    vs = [[i] * SUBLANES for i in range(SUBLANES)]
    for shift in (4, 2, 1):
        vs = [[vs[i][r] if (r % (2 * shift)) < shift else vs[i + 1][r - shift] for r in range(SUBLANES)]
              for i in range(0, len(vs), 2)]
    return vs[0]


_SLOT_PERM = _trace_fold_rows()


def _pack_rows(tab):
    n = tab.shape[0]
    t = tab.astype(BF16).reshape(n, SUBLANES // 2, 2, LANES).transpose(0, 1, 3, 2)
    return lax.bitcast_convert_type(t, jnp.uint32).reshape(n * (SUBLANES // 2), LANES)


def _gather_row(tab_ref, row4):
    words = tab_ref[pl.ds(pl.multiple_of(row4, SUBLANES // 2), SUBLANES // 2), :]
    return pltpu.bitcast(words, BF16).astype(F32)


def _split3(v):
    p0 = v.astype(BF16)
    r = v - p0.astype(F32)
    p1 = r.astype(BF16)
    p2 = (r - p1.astype(F32)).astype(BF16)
    return p0, p1, p2


def _peer_u_kernel(idx_ref, x_ref, tab_ref, g_ref, fold_ref, o_ref, a_scr, *stages):
    tg, n_sel = g_ref.shape
    group = len(stages)
    half = SUBLANES // 2
    n_rows = n_sel * SUBLANES
    rows = lax.broadcasted_iota(jnp.int32, (SUBLANES, n_rows), 0)
    cols = lax.broadcasted_iota(jnp.int32, (SUBLANES, n_rows), 1)
    same_row = rows == (cols & (SUBLANES - 1))

    def gather(t, s_ref):
        row = idx_ref.at[pl.ds(t * n_sel, n_sel)]
        for k in range(n_sel):
            s_ref[k * half:(k + 1) * half, :] = tab_ref[pl.ds(pl.multiple_of(row[k], half), half), :]

    def row_dots(t, s_ref):
        tiles = pltpu.bitcast(s_ref[...], BF16)
        xs = jnp.concatenate(_split3(x_ref[t]), axis=0)
        xs = jnp.concatenate([xs, jnp.zeros((SUBLANES, LANES), BF16)], axis=0)
        dots = lax.dot_general(xs, tiles, (((1,), (1,)), ((), ())), preferred_element_type=F32)
        per_piece = [jnp.where(same_row, dots[p * SUBLANES:(p + 1) * SUBLANES], 0.0) for p in range(3)]
        return (per_piece[0] + per_piece[1]) + per_piece[2]

    def step(i, carry):
        t0 = pl.multiple_of(i * group, group)
        for j in range(group):
            gather(t0 + j, stages[j])
        parts = [None] * group
        for j in range(group):
            parts[_SLOT_PERM[j]] = row_dots(t0 + j, stages[j])
        for shift in (4, 2, 1):
            parts = _fold_pairs(parts, shift)
        per_token = parts[0]
        pieces = jnp.concatenate(_split3(per_token) + (jnp.zeros((SUBLANES, n_rows), BF16),), axis=0)
        a = jnp.dot(pieces, fold_ref[...], preferred_element_type=F32)
        a_scr[pl.ds(t0, group), :] = (a[0:SUBLANES] + a[SUBLANES:2 * SUBLANES]) + a[2 * SUBLANES:3 * SUBLANES]
        return carry

    lax.fori_loop(0, tg // group, step, 0)
    o_ref[...] = g_ref[...] * jax.nn.gelu(a_scr[...])


def _peer_u(idx_flat, h3, tab, g_tm, fold):
    t, n_sel = g_tm.shape
    tg = PEER_U_TILE
    stage = pltpu.VMEM((n_sel * SUBLANES // 2, LANES), jnp.uint32)
    return pl.pallas_call(
        _peer_u_kernel,
        grid=(t // tg,),
        in_specs=[pl.BlockSpec((tg * n_sel,), lambda i: (i,), memory_space=pltpu.SMEM),
                  pl.BlockSpec((tg,) + h3.shape[1:], lambda i: (i, 0, 0)),
                  pl.BlockSpec(tab.shape, lambda i: (0, 0), pipeline_mode=pl.Buffered(1)),
                  pl.BlockSpec((tg, n_sel), lambda i: (i, 0)),
                  _resident(fold.shape)],
        out_specs=pl.BlockSpec((tg, n_sel), lambda i: (i, 0)),
        out_shape=jax.ShapeDtypeStruct((t, n_sel), F32),
        scratch_shapes=[pltpu.VMEM((tg, n_sel), F32)] + [stage] * SUBLANES,
        compiler_params=_cparams("arbitrary"),
        name="peer_u",
    )(idx_flat, h3, tab, g_tm, fold)


def _peer_v_kernel(idx_ref, w_ref, tab_ref, o_ref, wb_scr):
    n_sel, tg = w_ref.shape
    n_acc = 4
    for t in range(tg):
        wb_scr[t] = jnp.broadcast_to(w_ref[:, t:t + 1], (n_sel, LANES))

    def token(t, carry):
        row = idx_ref.at[pl.ds(t * n_sel, n_sel)]
        acc = [None] * n_acc
        for k in range(n_sel):
            term = _gather_row(tab_ref, row[k]) * wb_scr[t, k:k + 1, :]
            j = k % n_acc
            acc[j] = term if acc[j] is None else acc[j] + term
        o_ref[t] = (acc[0] + acc[1]) + (acc[2] + acc[3])
        return carry

    lax.fori_loop(0, tg, token, 0)


def _peer_v(idx_flat, w_km, tab):
    n_sel, t = w_km.shape
    tg = PEER_V_TILE
    return pl.pallas_call(
        _peer_v_kernel,
        grid=(t // tg,),
        in_specs=[pl.BlockSpec((tg * n_sel,), lambda i: (i,), memory_space=pltpu.SMEM),
                  pl.BlockSpec((n_sel, tg), lambda i: (0, i)),
                  pl.BlockSpec(tab.shape, lambda i: (0, 0), pipeline_mode=pl.Buffered(1))],
        out_specs=pl.BlockSpec((tg, SUBLANES, LANES), lambda i: (i, 0, 0)),
        out_shape=jax.ShapeDtypeStruct((t, SUBLANES, LANES), F32),
        scratch_shapes=[pltpu.VMEM((tg, n_sel, LANES), F32)],
        compiler_params=_cparams("arbitrary"),
        name="peer_v",
    )(idx_flat, w_km, tab)


def _residual_kernel(x_ref, y_ref, mod_ref, o_ref):
    o_ref[...] = x_ref[...] + mod_ref[0, 5:6, :] * y_ref[...]


def _residual(x, y, mods, tile_row, n_tokens):
    d = x.shape[1]
    tm = TOKEN_TILE
    row = lambda i: (i, 0)
    return pl.pallas_call(
        _residual_kernel,
        grid=(n_tokens // tm,),
        in_specs=[pl.BlockSpec((tm, d), row), pl.BlockSpec((tm, d), row),
                  pl.BlockSpec((1,) + mods.shape[1:], lambda i: (tile_row(i), 0, 0))],
        out_specs=pl.BlockSpec((tm, d), row),
        out_shape=jax.ShapeDtypeStruct((n_tokens, d), F32),
        compiler_params=_cparams("arbitrary"),
        name="residual",
    )(x, y, mods)


def _dft_tables(n):
    j = jnp.arange(n, dtype=jnp.int32)
    ang = (2.0 * math.pi / n) * ((j[:, None] * j[None, :]) % n).astype(F32)
    return jnp.cos(ang), jnp.sin(ang)


def _rope_tables(n):
    f = HEAD_DIM // 4
    rows = n // GRID_W
    row = jnp.repeat(jnp.arange(rows), GRID_W).astype(F32)
    col = jnp.tile(jnp.arange(GRID_W), rows).astype(F32)
    inv = ROPE_THETA ** (-jnp.arange(f, dtype=F32) / f)
    ar, ac = row[:, None] * inv, col[:, None] * inv
    cos = jnp.concatenate([jnp.cos(ar), jnp.cos(ar), jnp.cos(ac), jnp.cos(ac)], axis=1)
    sin = jnp.concatenate([-jnp.sin(ar), jnp.sin(ar), -jnp.sin(ac), jnp.sin(ac)], axis=1)
    return cos, sin


def _block_diag(blocks):
    g, r, c = blocks.shape
    out = jnp.zeros((g * r, g * c), blocks.dtype)
    for i in range(g):
        out = out.at[i * r:(i + 1) * r, i * c:(i + 1) * c].set(blocks[i])
    return out


def _seq_dft(fg, n_batch, n_seq, cs):
    w2 = fg.shape[1]
    w = w2 // 2
    g = fg.reshape(n_batch, n_seq, 2, w).transpose(2, 1, 0, 3).reshape(2 * n_seq, n_batch * w)
    tm = min(512, n_seq)
    re = _matmul(cs, g, tm, min(512, n_batch * w))
    return re.reshape(n_seq, n_batch, w).transpose(1, 0, 2).reshape(n_batch * n_seq, w)


def kernel(x, c, ctx, c_ctx, ada_w, ada_b, norm_mix, w_in, fourier_w, pool_w, pool_scale, pool_proj,
           q_norm, k_norm, attn_proj, w_out, norm_ffn, peer_wq, peer_keys, peer_u, peer_v):
    nb, n_lat, d = x.shape
    n_ctx = ctx.shape[1]
    depth = ada_w.shape[0]
    fw = fourier_w.shape[1]
    pw = pool_proj.shape[1]
    qw = attn_proj.shape[1]
    n_heads = qw // HEAD_DIM
    n_kv = n_heads // GQA_GROUP
    kw = n_kv * HEAD_DIM
    widths = (fw, pw, qw, kw)
    t_lat, t_ctx = nb * n_lat, nb * n_ctx
    t_all = t_lat + t_ctx
    n_exp = peer_u.shape[1]
    n_sel = PEER_HEADS * PEER_TOPK
    assert n_lat % TOKEN_TILE == 0 and n_ctx % TOKEN_TILE == 0 and t_all % PEER_U_TILE == 0
    assert d == SUBLANES * LANES

    lat_tiles = t_lat // TOKEN_TILE
    per_sample = n_lat // TOKEN_TILE
    tile_row = lambda i: jnp.where(i < lat_tiles, i // per_sample, nb)

    rows = -(-(nb + 1) // SUBLANES) * SUBLANES
    cvec = jnp.zeros((rows, d), F32).at[:nb].set(c).at[nb].set(c_ctx)
    mods_all = _adaln(cvec, ada_w, ada_b).reshape(depth, rows, 6, d)

    gd = fw // FOURIER_GROUPS
    cc, sc = _dft_tables(gd)
    norm = 1.0 / math.sqrt(gd)
    eye = jnp.eye(FOURIER_GROUPS, dtype=F32)
    dft_c = jnp.concatenate([jnp.kron(eye, cc), jnp.kron(eye, sc)], axis=1) * norm
    dft_c = dft_c.astype(BF16)

    def seq_tables(n):
        cl, sl = _dft_tables(n)
        return (jnp.concatenate([cl, -sl], axis=1) * (1.0 / math.sqrt(n))).astype(BF16)

    cs_lat, cs_ctx = seq_tables(n_lat), seq_tables(n_ctx)
    cos, sin = _rope_tables(n_lat)

    fold = jnp.repeat(jnp.eye(n_sel, dtype=BF16), SUBLANES, axis=0)

    stream = jnp.concatenate([x.reshape(t_lat, d), ctx.reshape(t_ctx, d)], axis=0)

    for l in range(depth):
        last = l == depth - 1
        mods = mods_all[l]
        w_in_b = w_in[l].astype(BF16)
        fg, p, q, k, v, gates = _in_proj(stream, mods, norm_mix[l][None, :], w_in_b, dft_c, tile_row, widths)

        re_lat = _seq_dft(fg[:t_lat], nb, n_lat, cs_lat)
        w_bd = _block_diag(pool_w[l]).astype(BF16)
        scale = pool_scale[l][None, :]
        mx_lat = _pool(p[:t_lat].reshape(nb, n_lat, pw), w_bd, scale).reshape(t_lat, pw)
        def heads(a, n, h):
            return a.reshape(nb, n, h, HEAD_DIM).transpose(0, 2, 1, 3)
        q_lat, q_ctx = heads(q[:t_lat], n_lat, n_heads), heads(q[t_lat:], n_ctx, n_heads)
        k_lat, k_ctx = heads(k[:t_lat], n_lat, n_kv), heads(k[t_lat:], n_ctx, n_kv)
        v_lat, v_ctx = heads(v[:t_lat], n_lat, n_kv), heads(v[t_lat:], n_ctx, n_kv)
        qg, kg = q_norm[l][None, :], k_norm[l][None, :]
        at_lat = _attention(q_lat, jnp.concatenate([k_lat, k_ctx], axis=2),
                            jnp.concatenate([v_lat, v_ctx], axis=2), cos, sin, qg, kg, n_lat)
        at_lat = at_lat.transpose(0, 2, 1, 3).reshape(t_lat, qw)

        if last:
            re, mixed, attn, n_tok = re_lat, mx_lat, at_lat, t_lat
        else:
            re_ctx = _seq_dft(fg[t_lat:], nb, n_ctx, cs_ctx)
            mx_ctx = _pool(p[t_lat:].reshape(nb, n_ctx, pw), w_bd, scale).reshape(t_ctx, pw)
            at_ctx = _attention(q_ctx, k_ctx, v_ctx, cos, sin, qg, kg, 0)
            at_ctx = at_ctx.transpose(0, 2, 1, 3).reshape(t_ctx, qw)
            re = jnp.concatenate([re_lat, re_ctx], axis=0)
            mixed = jnp.concatenate([mx_lat, mx_ctx], axis=0)
            attn = jnp.concatenate([at_lat, at_ctx], axis=0)
            n_tok = t_all

        x1 = _merge(re, mixed, attn, gates, stream, mods,
                    fourier_w[l].astype(BF16), pool_proj[l].astype(BF16),
                    attn_proj[l].astype(BF16), w_out[l].astype(BF16), tile_row, n_tok)

        keys_b = peer_keys[l].reshape(2 * PEER_HEADS, PEER_KEYS, -1).astype(BF16)
        h2, e_t, g_t = _peer_query(x1, mods, norm_ffn[l][None, :], peer_wq[l].astype(BF16), keys_b,
                                   tile_row, n_tok)
        idx4 = (e_t.T * (SUBLANES // 2)).reshape(-1)
        w_tm = _peer_u(idx4, h2.reshape(n_tok, SUBLANES, LANES), _pack_rows(peer_u[l]), g_t.T, fold)
        y = _peer_v(idx4, w_tm.T, _pack_rows(peer_v[l])).reshape(n_tok, d)
        stream = _residual(x1, y, mods, tile_row, n_tok)

    return stream[:t_lat].reshape(nb, n_lat, d)
```

```python
import functools
import math

import jax
import jax.numpy as jnp
from jax import lax
from jax.experimental import pallas as pl
from jax.experimental.pallas import tpu as pltpu

EPS = 1e-6
GRID_W = 64
HEAD_DIM = 64
GQA_GROUP = 4
ROPE_THETA = 10000.0
POOL_WINDOWS = (2, 4, 8, 16)
FOURIER_GROUPS = 4
PEER_HEADS = 8
PEER_KEYS = 128
PEER_TOPK = 16

LANES = 128
SUBLANES = 8
VMEM_LIMIT = 56 * 1024 * 1024

TOKEN_TILE = 256
ATTN_Q_TILE = 256
PEER_U_TILE = 128
PEER_V_TILE = 128
PEER_IDX_PARTS = 8

BF16 = jnp.bfloat16
F32 = jnp.float32
NEG_INF = float("-inf")


def _cparams(*sem):
    return pltpu.CompilerParams(dimension_semantics=sem, vmem_limit_bytes=VMEM_LIMIT)


def _rms(x, gain):
    return x * lax.rsqrt(jnp.mean(x * x, axis=-1, keepdims=True) + EPS) * gain


def _resident(shape):
    zeros = (0,) * len(shape)
    return pl.BlockSpec(shape, lambda *_: zeros)


def _adaln_kernel(c_ref, w_ref, b_ref, o_ref):
    c = c_ref[...]
    s = c * jax.nn.sigmoid(c)
    o_ref[0] = jnp.dot(s.astype(BF16), w_ref[0].astype(BF16), preferred_element_type=F32) + b_ref[0]


def _adaln(cvec, ada_w, ada_b):
    depth, d, six_d = ada_w.shape
    rows = cvec.shape[0]
    tn = 1024
    return pl.pallas_call(
        _adaln_kernel,
        grid=(depth, six_d // tn),
        in_specs=[pl.BlockSpec((rows, d), lambda l, j: (0, 0)),
                  pl.BlockSpec((1, d, tn), lambda l, j: (l, 0, j)),
                  pl.BlockSpec((1, 1, tn), lambda l, j: (l, 0, j))],
        out_specs=pl.BlockSpec((1, rows, tn), lambda l, j: (l, 0, j)),
        out_shape=jax.ShapeDtypeStruct((depth, rows, six_d), F32),
        compiler_params=_cparams("arbitrary", "arbitrary"),
        name="adaln",
    )(cvec, ada_w, ada_b.reshape(depth, 1, six_d))


def _proj_kernel(x_ref, mod_ref, g_ref, w_ref, dft_ref,
                 fg_ref, p_ref, q_ref, k_ref, v_ref, gate_ref, *, widths):
    fw, pw, qw, kw = widths
    x = x_ref[...]
    h = _rms(x, g_ref[...]) * (1.0 + mod_ref[0, 1:2, :]) + mod_ref[0, 0:1, :]
    hb = h.astype(BF16)

    def mm(c0, c1):
        return jnp.dot(hb, w_ref[:, c0:c1], preferred_element_type=F32)

    c = 0
    f = mm(c, c + fw); c += fw
    fg_ref[...] = jnp.dot(f.astype(BF16), dft_ref[...], preferred_element_type=F32).astype(BF16)
    p_ref[...] = mm(c, c + pw); c += pw
    q_ref[...] = mm(c, c + qw); c += qw
    k_ref[...] = mm(c, c + kw); c += kw
    v_ref[...] = mm(c, c + kw); c += kw
    n_gate = gate_ref.shape[1]
    step = 1024
    for j in range(0, n_gate, step):
        gate_ref[:, j:j + step] = jax.nn.sigmoid(mm(c + j, c + j + step))


def _in_proj(x, mods, gain, w_in_b, dft_c, tile_row, widths):
    t, d = x.shape
    fw, pw, qw, kw = widths
    n_gate = w_in_b.shape[1] - (fw + pw + qw + 2 * kw)
    tm = TOKEN_TILE
    row = lambda i: (i, 0)
    outs = [(2 * fw, BF16), (pw, F32), (qw, F32), (kw, F32), (kw, F32), (n_gate, F32)]
    return pl.pallas_call(
        functools.partial(_proj_kernel, widths=widths),
        grid=(t // tm,),
        in_specs=[pl.BlockSpec((tm, d), row),
                  pl.BlockSpec((1,) + mods.shape[1:], lambda i: (tile_row(i), 0, 0)),
                  _resident(gain.shape), _resident(w_in_b.shape), _resident(dft_c.shape)],
        out_specs=[pl.BlockSpec((tm, w), row) for w, _ in outs],
        out_shape=[jax.ShapeDtypeStruct((t, w), dt) for w, dt in outs],
        compiler_params=_cparams("arbitrary"),
        name="in_proj",
    )(x, mods, gain, w_in_b, dft_c)


def _mm_kernel(a_ref, b_ref, o_ref):
    o_ref[...] = jnp.dot(a_ref[...], b_ref[...], preferred_element_type=F32)


def _matmul(a, b, tm, tn):
    m, k = a.shape
    n = b.shape[1]
    return pl.pallas_call(
        _mm_kernel,
        grid=(m // tm, n // tn),
        in_specs=[pl.BlockSpec((tm, k), lambda i, j: (i, 0)),
                  pl.BlockSpec((k, tn), lambda i, j: (0, j))],
        out_specs=pl.BlockSpec((tm, tn), lambda i, j: (i, j)),
        out_shape=jax.ShapeDtypeStruct((m, n), F32),
        compiler_params=_cparams("arbitrary", "arbitrary"),
        name="seq_dft",
    )(a, b)


def _pool_kernel(p_ref, w_ref, s_ref, o_ref):
    x = p_ref[0]
    n, width = x.shape
    gdim = width // len(POOL_WINDOWS)
    t = lax.broadcasted_iota(jnp.int32, (n, 1), 0)
    col = lax.broadcasted_iota(jnp.int32, (1, width), 1)
    half = jnp.zeros((1, width), jnp.int32)
    for gi, win in enumerate(POOL_WINDOWS):
        half = jnp.where((col >= gi * gdim) & (col < (gi + 1) * gdim), win // 2, half)
    acc = jnp.zeros_like(x)
    max_half = max(POOL_WINDOWS) // 2
    for d in range(-max_half, max_half):
        shifted = x if d == 0 else pltpu.roll(x, (-d) % n, axis=0)
        row_ok = (t + d >= 0) & (t + d < n)
        col_ok = (half >= -d) if d < 0 else (half > d)
        acc = acc + jnp.where(row_ok, jnp.where(col_ok, shifted, 0.0), 0.0)
    cnt = (jnp.minimum(t + half, n) - jnp.maximum(t - half, 0)).astype(F32)
    pooled = acc / cnt - x
    o_ref[0] = jnp.dot(pooled.astype(BF16), w_ref[...], preferred_element_type=F32) * s_ref[...]


def _pool(p3, w_bd, scale):
    b, n, width = p3.shape
    blk = pl.BlockSpec((1, n, width), lambda i: (i, 0, 0))
    return pl.pallas_call(
        _pool_kernel,
        grid=(b,),
        in_specs=[blk, _resident(w_bd.shape), _resident(scale.shape)],
        out_specs=blk,
        out_shape=jax.ShapeDtypeStruct(p3.shape, F32),
        compiler_params=_cparams("arbitrary"),
        name="pool",
    )(p3, w_bd, scale)


def _swap_rope_halves(x):
    q = HEAD_DIM // 4
    return jnp.concatenate([x[:, q:2 * q], x[:, 0:q], x[:, 3 * q:4 * q], x[:, 2 * q:3 * q]], axis=1)


def _attn_kernel(q_ref, k_ref, v_ref, cos_ref, sin_ref, qn_ref, kn_ref, o_ref, kb_ref, vb_ref,
                 *, n_rope, tq):
    n_q = q_ref.shape[2]
    n_k = k_ref.shape[2]
    kn = _rms(k_ref[0, 0], kn_ref[...])
    if n_rope:
        k_lat = kn[:n_rope]
        kb_ref[0:n_rope, :] = (k_lat * cos_ref[...] + _swap_rope_halves(k_lat) * sin_ref[...]).astype(BF16)
        if n_k > n_rope:
            kb_ref[n_rope:n_k, :] = kn[n_rope:].astype(BF16)
    else:
        kb_ref[...] = kn.astype(BF16)
    vb_ref[...] = v_ref[0, 0].astype(BF16)
    n_tiles = n_q // tq

    def block(i, carry):
        g = i // n_tiles
        r0 = pl.multiple_of((i % n_tiles) * tq, tq)
        qn = _rms(q_ref[0, g, pl.ds(r0, tq), :], qn_ref[...])
        if n_rope:
            qn = qn * cos_ref[pl.ds(r0, tq), :] + _swap_rope_halves(qn) * sin_ref[pl.ds(r0, tq), :]
        qb = (qn * (HEAD_DIM ** -0.5)).astype(BF16)
        s = lax.dot_general(qb, kb_ref[...], (((1,), (1,)), ((), ())), preferred_element_type=F32)
        m = jnp.max(s, axis=-1, keepdims=True)
        p = jnp.exp(s - m)
        l = jnp.sum(p, axis=-1, keepdims=True)
        o = jnp.dot(p.astype(BF16), vb_ref[...], preferred_element_type=F32)
        o_ref[0, g, pl.ds(r0, tq), :] = o / l
        return carry

    lax.fori_loop(0, GQA_GROUP * n_tiles, block, 0)


def _attention(q4, k4, v4, cos, sin, q_gain, k_gain, n_rope):
    b, n_heads, n_q, hd = q4.shape
    n_kv = k4.shape[1]
    n_k = k4.shape[2]
    tq = min(ATTN_Q_TILE, n_q)
    qblk = pl.BlockSpec((1, GQA_GROUP, n_q, hd), lambda i, j: (i, j, 0, 0))
    kblk = pl.BlockSpec((1, 1, n_k, hd), lambda i, j: (i, j, 0, 0))
    return pl.pallas_call(
        functools.partial(_attn_kernel, n_rope=n_rope, tq=tq),
        grid=(b, n_kv),
        in_specs=[qblk, kblk, kblk, _resident(cos.shape), _resident(sin.shape),
                  _resident(q_gain.shape), _resident(k_gain.shape)],
        out_specs=qblk,
        out_shape=jax.ShapeDtypeStruct(q4.shape, F32),
        scratch_shapes=[pltpu.VMEM((n_k, hd), BF16), pltpu.VMEM((n_k, hd), BF16)],
        compiler_params=_cparams("arbitrary", "arbitrary"),
        name="attention",
    )(q4, k4, v4, cos, sin, q_gain, k_gain)


def _merge_kernel(re_ref, mx_ref, at_ref, gate_ref, x_ref, mod_ref, fw_ref, pw_ref, aw_ref, wo_ref, o_ref):
    d = x_ref.shape[1]
    yf = jnp.dot(re_ref[...].astype(BF16), fw_ref[...], preferred_element_type=F32)
    yp = jnp.dot(mx_ref[...].astype(BF16), pw_ref[...], preferred_element_type=F32)
    ya = jnp.dot(at_ref[...].astype(BF16), aw_ref[...], preferred_element_type=F32)
    z = gate_ref[:, 0:d] * yf + gate_ref[:, d:2 * d] * yp + gate_ref[:, 2 * d:3 * d] * ya
    y = jnp.dot(z.astype(BF16), wo_ref[...], preferred_element_type=F32)
    o_ref[...] = x_ref[...] + mod_ref[0, 2:3, :] * y


def _merge(re, mixed, attn, gates, x, mods, fw, pw, aw, wo, tile_row, n_tokens):
    d = x.shape[1]
    tm = TOKEN_TILE
    row = lambda i: (i, 0)
    ins = [re, mixed, attn, gates, x]
    return pl.pallas_call(
        _merge_kernel,
        grid=(n_tokens // tm,),
        in_specs=[pl.BlockSpec((tm, a.shape[1]), row) for a in ins]
        + [pl.BlockSpec((1,) + mods.shape[1:], lambda i: (tile_row(i), 0, 0))]
        + [_resident(w.shape) for w in (fw, pw, aw, wo)],
        out_specs=pl.BlockSpec((tm, d), row),
        out_shape=jax.ShapeDtypeStruct((n_tokens, d), F32),
        compiler_params=_cparams("arbitrary"),
        name="merge",
    )(*ins, mods, fw, pw, aw, wo)


def _topk_rows(s, k):
    n = s.shape[0]
    rows = lax.broadcasted_iota(jnp.int32, s.shape, 0)
    vals, ids = [], []
    for _ in range(k):
        m = jnp.max(s, axis=0, keepdims=True)
        r = jnp.min(jnp.where(s == m, rows, n), axis=0, keepdims=True)
        vals.append(m)
        ids.append(r)
        s = jnp.where(rows == r, NEG_INF, s)
    return jnp.concatenate(vals, axis=0), jnp.concatenate(ids, axis=0)


def _pair_candidates(s1, i1, s2, i2):
    k = PEER_TOPK
    row8 = lax.broadcasted_iota(jnp.int32, (SUBLANES, 1), 0)
    cand, expert = [s1[0:1] + s2], [i1[0:1] * PEER_KEYS + i2]
    for a in range(1, SUBLANES):
        nb = k // (a + 1)
        c = s1[a:a + 1] + s2[0:SUBLANES]
        cand.append(jnp.where(row8 < nb, c, NEG_INF))
        expert.append(i1[a:a + 1] * PEER_KEYS + i2[0:SUBLANES])
    cand.append(s1[SUBLANES:k] + s2[0:1])
    expert.append(i1[SUBLANES:k] * PEER_KEYS + i2[0:1])
    return jnp.concatenate(cand, axis=0), jnp.concatenate(expert, axis=0)


def _peer_q_kernel(x_ref, mod_ref, g_ref, wq_ref, keys_ref, h_ref, e_ref, gw_ref, q_scr):
    kd = keys_ref.shape[2]
    h = _rms(x_ref[...], g_ref[...]) * (1.0 + mod_ref[0, 4:5, :]) + mod_ref[0, 3:4, :]
    h_ref[...] = h
    hb = h.astype(BF16)
    for j in range(2 * PEER_HEADS):
        q_scr[j] = jnp.dot(hb, wq_ref[:, j * kd:(j + 1) * kd], preferred_element_type=F32).astype(BF16)

    def head(hh, carry):
        tops = []
        for p in range(2):
            s = lax.dot_general(keys_ref[2 * hh + p], q_scr[2 * hh + p], (((1,), (1,)), ((), ())),
                                preferred_element_type=F32)
            tops.append(_topk_rows(s, PEER_TOPK))
        cand, expert = _pair_candidates(tops[0][0], tops[0][1], tops[1][0], tops[1][1])
        n = cand.shape[0]
        rows = lax.broadcasted_iota(jnp.int32, cand.shape, 0)
        top, eid = [], []
        for _ in range(PEER_TOPK):
            m = jnp.max(cand, axis=0, keepdims=True)
            r = jnp.min(jnp.where(cand == m, rows, n), axis=0, keepdims=True)
            sel = rows == r
            top.append(m)
            eid.append(jnp.max(jnp.where(sel, expert, -1), axis=0, keepdims=True))
            cand = jnp.where(sel, NEG_INF, cand)
        top = jnp.concatenate(top, axis=0)
        ex = jnp.exp(top - top[0:1])
        r0 = pl.multiple_of(hh * PEER_TOPK, PEER_TOPK)
        gw_ref[pl.ds(r0, PEER_TOPK), :] = ex / jnp.sum(ex, axis=0, keepdims=True)
        e_ref[pl.ds(r0, PEER_TOPK), :] = jnp.concatenate(eid, axis=0)
        return carry

    lax.fori_loop(0, PEER_HEADS, head, 0)


def _peer_query(x, mods, gain, wq_b, keys_b, tile_row, n_tokens):
    d = x.shape[1]
    tm = TOKEN_TILE
    n_sel = PEER_HEADS * PEER_TOPK
    kd = keys_b.shape[2]
    return pl.pallas_call(
        _peer_q_kernel,
        grid=(n_tokens // tm,),
        in_specs=[pl.BlockSpec((tm, d), lambda i: (i, 0)),
                  pl.BlockSpec((1,) + mods.shape[1:], lambda i: (tile_row(i), 0, 0)),
                  _resident(gain.shape), _resident(wq_b.shape), _resident(keys_b.shape)],
        out_specs=[pl.BlockSpec((tm, d), lambda i: (i, 0)),
                   pl.BlockSpec((n_sel, tm), lambda i: (0, i)),
                   pl.BlockSpec((n_sel, tm), lambda i: (0, i))],
        out_shape=[jax.ShapeDtypeStruct((n_tokens, d), F32),
                   jax.ShapeDtypeStruct((n_sel, n_tokens), jnp.int32),
                   jax.ShapeDtypeStruct((n_sel, n_tokens), F32)],
        scratch_shapes=[pltpu.VMEM((2 * PEER_HEADS, tm, kd), BF16)],
        compiler_params=_cparams("arbitrary"),
        name="peer_query",
    )(x, mods, gain, wq_b, keys_b)


def _fold_pairs(vs, shift):
    sub = lax.broadcasted_iota(jnp.int32, vs[0].shape, 0)
    keep = (sub % (2 * shift)) < shift
    out = []
    for i in range(0, len(vs), 2):
        a = vs[i] + pltpu.roll(vs[i], SUBLANES - shift, axis=0)
        b = vs[i + 1] + pltpu.roll(vs[i + 1], SUBLANES - shift, axis=0)
        out.append(jnp.where(keep, a, pltpu.roll(b, shift, axis=0)))
    return out


def _trace_fold_rows():
    vs = [[i] * SUBLANES for i in range(SUBLANES)]
    for shift in (4, 2, 1):
        vs = [[vs[i][r] if (r % (2 * shift)) < shift else vs[i + 1][r - shift] for r in range(SUBLANES)]
              for i in range(0, len(vs), 2)]
    return vs[0]


_SLOT_PERM = _trace_fold_rows()


def _pack_rows(tab):
    n = tab.shape[0]
    t = tab.astype(BF16).reshape(n, SUBLANES // 2, 2, LANES).transpose(0, 1, 3, 2)
    return lax.bitcast_convert_type(t, jnp.uint32).reshape(n * (SUBLANES // 2), LANES)


def _gather_row(tab_ref, row4):
    words = tab_ref[pl.ds(pl.multiple_of(row4, SUBLANES // 2), SUBLANES // 2), :]
    return pltpu.bitcast(words, BF16).astype(F32)


def _split3(v):
    p0 = v.astype(BF16)
    r = v - p0.astype(F32)
    p1 = r.astype(BF16)
    p2 = (r - p1.astype(F32)).astype(BF16)
    return p0, p1, p2


def _peer_u_kernel(*refs):
    idx_refs = refs[:PEER_IDX_PARTS]
    x_ref, tab_ref, g_ref, fold_ref, o_ref, a_scr, at_scr = refs[PEER_IDX_PARTS:PEER_IDX_PARTS + 7]
    stages = refs[PEER_IDX_PARTS + 7:]
    tg, n_sel = g_ref.shape
    group = len(stages)
    per_part = n_sel // PEER_IDX_PARTS
    mxu_parts = PEER_IDX_PARTS // 2
    n_half = n_sel // 2
    half = SUBLANES // 2
    n_rows = n_half * SUBLANES
    rows = lax.broadcasted_iota(jnp.int32, (SUBLANES, n_rows), 0)
    cols = lax.broadcasted_iota(jnp.int32, (SUBLANES, n_rows), 1)
    same_row = rows == (cols & (SUBLANES - 1))
    lane = lax.broadcasted_iota(jnp.int32, (n_half, LANES), 1)
    at_scr[...] = jnp.zeros_like(at_scr)

    def stage_tiles(t, s_ref):
        part_rows = [r.at[pl.ds(t * per_part, per_part)] for r in idx_refs[:mxu_parts]]
        for j in range(per_part):
            for q in range(mxu_parts):
                k = q * per_part + j
                s_ref[k * half:(k + 1) * half, :] = tab_ref[pl.ds(pl.multiple_of(part_rows[q][j], half), half), :]

    def row_dots(t, s_ref):
        tiles = pltpu.bitcast(s_ref[...], BF16)
        xs = jnp.concatenate(_split3(x_ref[t]), axis=0)
        xs = jnp.concatenate([xs, jnp.zeros((SUBLANES, LANES), BF16)], axis=0)
        dots = lax.dot_general(xs, tiles, (((1,), (1,)), ((), ())), preferred_element_type=F32)
        per_piece = [jnp.where(same_row, dots[p * SUBLANES:(p + 1) * SUBLANES], 0.0) for p in range(3)]
        return (per_piece[0] + per_piece[1]) + per_piece[2]

    def lane_partials(t):
        xt = x_ref[t]
        part_rows = [r.at[pl.ds(t * per_part, per_part)] for r in idx_refs[mxu_parts:]]
        prods = [None] * n_half
        for j in range(per_part):
            for q in range(mxu_parts):
                f, s = divmod(q * per_part + j, SUBLANES)
                prods[f * SUBLANES + _SLOT_PERM[s]] = _gather_row(tab_ref, part_rows[q][j]) * xt
        for shift in (4, 2, 1):
            prods = _fold_pairs(prods, shift)
        return jnp.concatenate(prods, axis=0)

    def step(i, carry):
        t0 = pl.multiple_of(i * group, group)
        for j in range(group):
            stage_tiles(t0 + j, stages[j])
        at = at_scr[0:n_half, :]
        for j in range(group):
            col = jnp.sum(lane_partials(t0 + j), axis=1, keepdims=True)
            at = jnp.where(lane == t0 + j, col, at)
        at_scr[0:n_half, :] = at
        parts = [None] * group
        for j in range(group):
            parts[_SLOT_PERM[j]] = row_dots(t0 + j, stages[j])
        for shift in (4, 2, 1):
            parts = _fold_pairs(parts, shift)
        per_token = parts[0]
        pieces = jnp.concatenate(_split3(per_token) + (jnp.zeros((SUBLANES, n_rows), BF16),), axis=0)
        a = jnp.dot(pieces, fold_ref[...], preferred_element_type=F32)
        a_scr[pl.ds(t0, group), :] = (a[0:SUBLANES] + a[SUBLANES:2 * SUBLANES]) + a[2 * SUBLANES:3 * SUBLANES]
        return carry

    lax.fori_loop(0, tg // group, step, 0)
    a_all = jnp.concatenate([a_scr[...], at_scr[...].T[:, 0:n_half]], axis=1)
    o_ref[...] = g_ref[...] * jax.nn.gelu(a_all)


def _peer_u(idx_parts, h3, tab, g_tm, fold):
    t, n_sel = g_tm.shape
    tg = PEER_U_TILE
    assert tg == LANES and PEER_IDX_PARTS % 2 == 0
    per_tile = tg * n_sel // PEER_IDX_PARTS
    stage = pltpu.VMEM((n_sel // 2 * SUBLANES // 2, LANES), jnp.uint32)
    return pl.pallas_call(
        _peer_u_kernel,
        grid=(t // tg,),
        in_specs=[pl.BlockSpec((per_tile,), lambda i: (i,), memory_space=pltpu.SMEM) for _ in idx_parts]
        + [pl.BlockSpec((tg,) + h3.shape[1:], lambda i: (i, 0, 0)),
           pl.BlockSpec(tab.shape, lambda i: (0, 0), pipeline_mode=pl.Buffered(1)),
           pl.BlockSpec((tg, n_sel), lambda i: (i, 0)),
           _resident(fold.shape)],
        out_specs=pl.BlockSpec((tg, n_sel), lambda i: (i, 0)),
        out_shape=jax.ShapeDtypeStruct((t, n_sel), F32),
        scratch_shapes=[pltpu.VMEM((tg, n_sel // 2), F32), pltpu.VMEM((LANES, LANES), F32)] + [stage] * SUBLANES,
        compiler_params=_cparams("arbitrary"),
        name="peer_u",
    )(*idx_parts, h3, tab, g_tm, fold)


def _split_index(idx_tm, parts):
    t, n = idx_tm.shape
    split = idx_tm.reshape(t, parts, n // parts)
    return [split[:, q, :].reshape(-1) for q in range(parts)]


def _peer_v_kernel(*refs):
    idx_refs = refs[:PEER_IDX_PARTS]
    w_ref, tab_ref, o_ref, wb_scr = refs[PEER_IDX_PARTS:]
    n_sel, tg = w_ref.shape
    per_part = n_sel // PEER_IDX_PARTS
    n_acc = 4
    for t in range(tg):
        wb_scr[t] = jnp.broadcast_to(w_ref[:, t:t + 1], (n_sel, LANES))

    def token(t, carry):
        rows = [r.at[pl.ds(t * per_part, per_part)] for r in idx_refs]
        acc = [None] * n_acc
        for j in range(per_part):
            for q in range(PEER_IDX_PARTS):
                k = q * per_part + j
                term = _gather_row(tab_ref, rows[q][j]) * wb_scr[t, k:k + 1, :]
                a = k % n_acc
                acc[a] = term if acc[a] is None else acc[a] + term
        o_ref[t] = (acc[0] + acc[1]) + (acc[2] + acc[3])
        return carry

    lax.fori_loop(0, tg, token, 0)


def _peer_v(idx_parts, w_km, tab):
    n_sel, t = w_km.shape
    tg = PEER_V_TILE
    per_tile = tg * n_sel // PEER_IDX_PARTS
    return pl.pallas_call(
        _peer_v_kernel,
        grid=(t // tg,),
        in_specs=[pl.BlockSpec((per_tile,), lambda i: (i,), memory_space=pltpu.SMEM) for _ in idx_parts]
        + [pl.BlockSpec((n_sel, tg), lambda i: (0, i)),
           pl.BlockSpec(tab.shape, lambda i: (0, 0), pipeline_mode=pl.Buffered(1))],
        out_specs=pl.BlockSpec((tg, SUBLANES, LANES), lambda i: (i, 0, 0)),
        out_shape=jax.ShapeDtypeStruct((t, SUBLANES, LANES), F32),
        scratch_shapes=[pltpu.VMEM((tg, n_sel, LANES), F32)],
        compiler_params=_cparams("arbitrary"),
        name="peer_v",
    )(*idx_parts, w_km, tab)


def _residual_kernel(x_ref, y_ref, mod_ref, o_ref):
    o_ref[...] = x_ref[...] + mod_ref[0, 5:6, :] * y_ref[...]


def _residual(x, y, mods, tile_row, n_tokens):
    d = x.shape[1]
    tm = TOKEN_TILE
    row = lambda i: (i, 0)
    return pl.pallas_call(
        _residual_kernel,
        grid=(n_tokens // tm,),
        in_specs=[pl.BlockSpec((tm, d), row), pl.BlockSpec((tm, d), row),
                  pl.BlockSpec((1,) + mods.shape[1:], lambda i: (tile_row(i), 0, 0))],
        out_specs=pl.BlockSpec((tm, d), row),
        out_shape=jax.ShapeDtypeStruct((n_tokens, d), F32),
        compiler_params=_cparams("arbitrary"),
        name="residual",
    )(x, y, mods)


def _dft_tables(n):
    j = jnp.arange(n, dtype=jnp.int32)
    ang = (2.0 * math.pi / n) * ((j[:, None] * j[None, :]) % n).astype(F32)
    return jnp.cos(ang), jnp.sin(ang)


def _rope_tables(n):
    f = HEAD_DIM // 4
    rows = n // GRID_W
    row = jnp.repeat(jnp.arange(rows), GRID_W).astype(F32)
    col = jnp.tile(jnp.arange(GRID_W), rows).astype(F32)
    inv = ROPE_THETA ** (-jnp.arange(f, dtype=F32) / f)
    ar, ac = row[:, None] * inv, col[:, None] * inv
    cos = jnp.concatenate([jnp.cos(ar), jnp.cos(ar), jnp.cos(ac), jnp.cos(ac)], axis=1)
    sin = jnp.concatenate([-jnp.sin(ar), jnp.sin(ar), -jnp.sin(ac), jnp.sin(ac)], axis=1)
    return cos, sin


def _block_diag(blocks):
    g, r, c = blocks.shape
    out = jnp.zeros((g * r, g * c), blocks.dtype)
    for i in range(g):
        out = out.at[i * r:(i + 1) * r, i * c:(i + 1) * c].set(blocks[i])
    return out


def _seq_dft(fg, n_batch, n_seq, cs):
    w2 = fg.shape[1]
    w = w2 // 2
    g = fg.reshape(n_batch, n_seq, 2, w).transpose(2, 1, 0, 3).reshape(2 * n_seq, n_batch * w)
    tm = min(512, n_seq)
    re = _matmul(cs, g, tm, min(512, n_batch * w))
    return re.reshape(n_seq, n_batch, w).transpose(1, 0, 2).reshape(n_batch * n_seq, w)


def kernel(x, c, ctx, c_ctx, ada_w, ada_b, norm_mix, w_in, fourier_w, pool_w, pool_scale, pool_proj,
           q_norm, k_norm, attn_proj, w_out, norm_ffn, peer_wq, peer_keys, peer_u, peer_v):
    nb, n_lat, d = x.shape
    n_ctx = ctx.shape[1]
    depth = ada_w.shape[0]
    fw = fourier_w.shape[1]
    pw = pool_proj.shape[1]
    qw = attn_proj.shape[1]
    n_heads = qw // HEAD_DIM
    n_kv = n_heads // GQA_GROUP
    kw = n_kv * HEAD_DIM
    widths = (fw, pw, qw, kw)
    t_lat, t_ctx = nb * n_lat, nb * n_ctx
    t_all = t_lat + t_ctx
    n_exp = peer_u.shape[1]
    n_sel = PEER_HEADS * PEER_TOPK
    assert n_lat % TOKEN_TILE == 0 and n_ctx % TOKEN_TILE == 0 and t_all % PEER_U_TILE == 0
    assert d == SUBLANES * LANES

    lat_tiles = t_lat // TOKEN_TILE
    per_sample = n_lat // TOKEN_TILE
    tile_row = lambda i: jnp.where(i < lat_tiles, i // per_sample, nb)

    rows = -(-(nb + 1) // SUBLANES) * SUBLANES
    cvec = jnp.zeros((rows, d), F32).at[:nb].set(c).at[nb].set(c_ctx)
    mods_all = _adaln(cvec, ada_w, ada_b).reshape(depth, rows, 6, d)

    gd = fw // FOURIER_GROUPS
    cc, sc = _dft_tables(gd)
    norm = 1.0 / math.sqrt(gd)
    eye = jnp.eye(FOURIER_GROUPS, dtype=F32)
    dft_c = jnp.concatenate([jnp.kron(eye, cc), jnp.kron(eye, sc)], axis=1) * norm
    dft_c = dft_c.astype(BF16)

    def seq_tables(n):
        cl, sl = _dft_tables(n)
        return (jnp.concatenate([cl, -sl], axis=1) * (1.0 / math.sqrt(n))).astype(BF16)

    cs_lat, cs_ctx = seq_tables(n_lat), seq_tables(n_ctx)
    cos, sin = _rope_tables(n_lat)

    fold = jnp.repeat(jnp.eye(n_sel // 2, dtype=BF16), SUBLANES, axis=0)

    stream = jnp.concatenate([x.reshape(t_lat, d), ctx.reshape(t_ctx, d)], axis=0)

    for l in range(depth):
        last = l == depth - 1
        mods = mods_all[l]
        w_in_b = w_in[l].astype(BF16)
        fg, p, q, k, v, gates = _in_proj(stream, mods, norm_mix[l][None, :], w_in_b, dft_c, tile_row, widths)

        re_lat = _seq_dft(fg[:t_lat], nb, n_lat, cs_lat)
        w_bd = _block_diag(pool_w[l]).astype(BF16)
        scale = pool_scale[l][None, :]
        mx_lat = _pool(p[:t_lat].reshape(nb, n_lat, pw), w_bd, scale).reshape(t_lat, pw)
        def heads(a, n, h):
            return a.reshape(nb, n, h, HEAD_DIM).transpose(0, 2, 1, 3)
        q_lat, q_ctx = heads(q[:t_lat], n_lat, n_heads), heads(q[t_lat:], n_ctx, n_heads)
        k_lat, k_ctx = heads(k[:t_lat], n_lat, n_kv), heads(k[t_lat:], n_ctx, n_kv)
        v_lat, v_ctx = heads(v[:t_lat], n_lat, n_kv), heads(v[t_lat:], n_ctx, n_kv)
        qg, kg = q_norm[l][None, :], k_norm[l][None, :]
        at_lat = _attention(q_lat, jnp.concatenate([k_lat, k_ctx], axis=2),
                            jnp.concatenate([v_lat, v_ctx], axis=2), cos, sin, qg, kg, n_lat)
        at_lat = at_lat.transpose(0, 2, 1, 3).reshape(t_lat, qw)

        if last:
            re, mixed, attn, n_tok = re_lat, mx_lat, at_lat, t_lat
        else:
            re_ctx = _seq_dft(fg[t_lat:], nb, n_ctx, cs_ctx)
            mx_ctx = _pool(p[t_lat:].reshape(nb, n_ctx, pw), w_bd, scale).reshape(t_ctx, pw)
            at_ctx = _attention(q_ctx, k_ctx, v_ctx, cos, sin, qg, kg, 0)
            at_ctx = at_ctx.transpose(0, 2, 1, 3).reshape(t_ctx, qw)
            re = jnp.concatenate([re_lat, re_ctx], axis=0)
            mixed = jnp.concatenate([mx_lat, mx_ctx], axis=0)
            attn = jnp.concatenate([at_lat, at_ctx], axis=0)
            n_tok = t_all

        x1 = _merge(re, mixed, attn, gates, stream, mods,
                    fourier_w[l].astype(BF16), pool_proj[l].astype(BF16),
                    attn_proj[l].astype(BF16), w_out[l].astype(BF16), tile_row, n_tok)

        keys_b = peer_keys[l].reshape(2 * PEER_HEADS, PEER_KEYS, -1).astype(BF16)
        h2, e_t, g_t = _peer_query(x1, mods, norm_ffn[l][None, :], peer_wq[l].astype(BF16), keys_b,
                                   tile_row, n_tok)
        idx_parts = _split_index(e_t.T * (SUBLANES // 2), PEER_IDX_PARTS)
        w_tm = _peer_u(idx_parts, h2.reshape(n_tok, SUBLANES, LANES), _pack_rows(peer_u[l]), g_t.T, fold)
        y = _peer_v(idx_parts, w_tm.T, _pack_rows(peer_v[l])).reshape(n_tok, d)
        stream = _residual(x1, y, mods, tile_row, n_tok)

    return stream[:t_lat].reshape(nb, n_lat, d)
```

```python
import functools
import math

import jax
import jax.numpy as jnp
from jax import lax
from jax.experimental import pallas as pl
from jax.experimental.pallas import tpu as pltpu

EPS = 1e-6
GRID_W = 64
HEAD_DIM = 64
GQA_GROUP = 4
ROPE_THETA = 10000.0
POOL_WINDOWS = (2, 4, 8, 16)
FOURIER_GROUPS = 4
PEER_HEADS = 8
PEER_KEYS = 128
PEER_TOPK = 16

LANES = 128
SUBLANES = 8
VMEM_LIMIT = 56 * 1024 * 1024

TOKEN_TILE = 256
ATTN_Q_TILE = 256
PEER_U_TILE = 128
PEER_V_TILE = 128
PEER_IDX_PARTS = 8
PEER_U_MXU_PARTS = 3

BF16 = jnp.bfloat16
F32 = jnp.float32
NEG_INF = float("-inf")


def _cparams(*sem):
    return pltpu.CompilerParams(dimension_semantics=sem, vmem_limit_bytes=VMEM_LIMIT)


def _rms(x, gain):
    return x * lax.rsqrt(jnp.mean(x * x, axis=-1, keepdims=True) + EPS) * gain


def _resident(shape):
    zeros = (0,) * len(shape)
    return pl.BlockSpec(shape, lambda *_: zeros)


def _adaln_kernel(c_ref, w_ref, b_ref, o_ref):
    c = c_ref[...]
    s = c * jax.nn.sigmoid(c)
    o_ref[0] = jnp.dot(s.astype(BF16), w_ref[0].astype(BF16), preferred_element_type=F32) + b_ref[0]


def _adaln(cvec, ada_w, ada_b):
    depth, d, six_d = ada_w.shape
    rows = cvec.shape[0]
    tn = 1024
    return pl.pallas_call(
        _adaln_kernel,
        grid=(depth, six_d // tn),
        in_specs=[pl.BlockSpec((rows, d), lambda l, j: (0, 0)),
                  pl.BlockSpec((1, d, tn), lambda l, j: (l, 0, j)),
                  pl.BlockSpec((1, 1, tn), lambda l, j: (l, 0, j))],
        out_specs=pl.BlockSpec((1, rows, tn), lambda l, j: (l, 0, j)),
        out_shape=jax.ShapeDtypeStruct((depth, rows, six_d), F32),
        compiler_params=_cparams("arbitrary", "arbitrary"),
        name="adaln",
    )(cvec, ada_w, ada_b.reshape(depth, 1, six_d))


def _proj_kernel(x_ref, mod_ref, g_ref, w_ref, dft_ref,
                 fg_ref, p_ref, q_ref, k_ref, v_ref, gate_ref, *, widths):
    fw, pw, qw, kw = widths
    x = x_ref[...]
    h = _rms(x, g_ref[...]) * (1.0 + mod_ref[0, 1:2, :]) + mod_ref[0, 0:1, :]
    hb = h.astype(BF16)

    def mm(c0, c1):
        return jnp.dot(hb, w_ref[:, c0:c1], preferred_element_type=F32)

    c = 0
    f = mm(c, c + fw); c += fw
    fg_ref[...] = jnp.dot(f.astype(BF16), dft_ref[...], preferred_element_type=F32).astype(BF16)
    p_ref[...] = mm(c, c + pw); c += pw
    q_ref[...] = mm(c, c + qw); c += qw
    k_ref[...] = mm(c, c + kw); c += kw
    v_ref[...] = mm(c, c + kw); c += kw
    n_gate = gate_ref.shape[1]
    step = 1024
    for j in range(0, n_gate, step):
        gate_ref[:, j:j + step] = jax.nn.sigmoid(mm(c + j, c + j + step))


def _in_proj(x, mods, gain, w_in_b, dft_c, tile_row, widths):
    t, d = x.shape
    fw, pw, qw, kw = widths
    n_gate = w_in_b.shape[1] - (fw + pw + qw + 2 * kw)
    tm = TOKEN_TILE
    row = lambda i: (i, 0)
    outs = [(2 * fw, BF16), (pw, F32), (qw, F32), (kw, F32), (kw, F32), (n_gate, F32)]
    return pl.pallas_call(
        functools.partial(_proj_kernel, widths=widths),
        grid=(t // tm,),
        in_specs=[pl.BlockSpec((tm, d), row),
                  pl.BlockSpec((1,) + mods.shape[1:], lambda i: (tile_row(i), 0, 0)),
                  _resident(gain.shape), _resident(w_in_b.shape), _resident(dft_c.shape)],
        out_specs=[pl.BlockSpec((tm, w), row) for w, _ in outs],
        out_shape=[jax.ShapeDtypeStruct((t, w), dt) for w, dt in outs],
        compiler_params=_cparams("arbitrary"),
        name="in_proj",
    )(x, mods, gain, w_in_b, dft_c)


def _mm_kernel(a_ref, b_ref, o_ref):
    o_ref[...] = jnp.dot(a_ref[...], b_ref[...], preferred_element_type=F32)


def _matmul(a, b, tm, tn):
    m, k = a.shape
    n = b.shape[1]
    return pl.pallas_call(
        _mm_kernel,
        grid=(m // tm, n // tn),
        in_specs=[pl.BlockSpec((tm, k), lambda i, j: (i, 0)),
                  pl.BlockSpec((k, tn), lambda i, j: (0, j))],
        out_specs=pl.BlockSpec((tm, tn), lambda i, j: (i, j)),
        out_shape=jax.ShapeDtypeStruct((m, n), F32),
        compiler_params=_cparams("arbitrary", "arbitrary"),
        name="seq_dft",
    )(a, b)


def _pool_kernel(p_ref, w_ref, s_ref, o_ref):
    x = p_ref[0]
    n, width = x.shape
    gdim = width // len(POOL_WINDOWS)
    t = lax.broadcasted_iota(jnp.int32, (n, 1), 0)
    col = lax.broadcasted_iota(jnp.int32, (1, width), 1)
    half = jnp.zeros((1, width), jnp.int32)
    for gi, win in enumerate(POOL_WINDOWS):
        half = jnp.where((col >= gi * gdim) & (col < (gi + 1) * gdim), win // 2, half)
    acc = jnp.zeros_like(x)
    max_half = max(POOL_WINDOWS) // 2
    for d in range(-max_half, max_half):
        shifted = x if d == 0 else pltpu.roll(x, (-d) % n, axis=0)
        row_ok = (t + d >= 0) & (t + d < n)
        col_ok = (half >= -d) if d < 0 else (half > d)
        acc = acc + jnp.where(row_ok, jnp.where(col_ok, shifted, 0.0), 0.0)
    cnt = (jnp.minimum(t + half, n) - jnp.maximum(t - half, 0)).astype(F32)
    pooled = acc / cnt - x
    o_ref[0] = jnp.dot(pooled.astype(BF16), w_ref[...], preferred_element_type=F32) * s_ref[...]


def _pool(p3, w_bd, scale):
    b, n, width = p3.shape
    blk = pl.BlockSpec((1, n, width), lambda i: (i, 0, 0))
    return pl.pallas_call(
        _pool_kernel,
        grid=(b,),
        in_specs=[blk, _resident(w_bd.shape), _resident(scale.shape)],
        out_specs=blk,
        out_shape=jax.ShapeDtypeStruct(p3.shape, F32),
        compiler_params=_cparams("arbitrary"),
        name="pool",
    )(p3, w_bd, scale)


def _swap_rope_halves(x):
    q = HEAD_DIM // 4
    return jnp.concatenate([x[:, q:2 * q], x[:, 0:q], x[:, 3 * q:4 * q], x[:, 2 * q:3 * q]], axis=1)


def _attn_kernel(q_ref, k_ref, v_ref, cos_ref, sin_ref, qn_ref, kn_ref, o_ref, kb_ref, vb_ref,
                 *, n_rope, tq):
    n_q = q_ref.shape[2]
    n_k = k_ref.shape[2]
    kn = _rms(k_ref[0, 0], kn_ref[...])
    if n_rope:
        k_lat = kn[:n_rope]
        kb_ref[0:n_rope, :] = (k_lat * cos_ref[...] + _swap_rope_halves(k_lat) * sin_ref[...]).astype(BF16)
        if n_k > n_rope:
            kb_ref[n_rope:n_k, :] = kn[n_rope:].astype(BF16)
    else:
        kb_ref[...] = kn.astype(BF16)
    vb_ref[...] = v_ref[0, 0].astype(BF16)
    n_tiles = n_q // tq

    def block(i, carry):
        g = i // n_tiles
        r0 = pl.multiple_of((i % n_tiles) * tq, tq)
        qn = _rms(q_ref[0, g, pl.ds(r0, tq), :], qn_ref[...])
        if n_rope:
            qn = qn * cos_ref[pl.ds(r0, tq), :] + _swap_rope_halves(qn) * sin_ref[pl.ds(r0, tq), :]
        qb = (qn * (HEAD_DIM ** -0.5)).astype(BF16)
        s = lax.dot_general(qb, kb_ref[...], (((1,), (1,)), ((), ())), preferred_element_type=F32)
        m = jnp.max(s, axis=-1, keepdims=True)
        p = jnp.exp(s - m)
        l = jnp.sum(p, axis=-1, keepdims=True)
        o = jnp.dot(p.astype(BF16), vb_ref[...], preferred_element_type=F32)
        o_ref[0, g, pl.ds(r0, tq), :] = o / l
        return carry

    lax.fori_loop(0, GQA_GROUP * n_tiles, block, 0)


def _attention(q4, k4, v4, cos, sin, q_gain, k_gain, n_rope):
    b, n_heads, n_q, hd = q4.shape
    n_kv = k4.shape[1]
    n_k = k4.shape[2]
    tq = min(ATTN_Q_TILE, n_q)
    qblk = pl.BlockSpec((1, GQA_GROUP, n_q, hd), lambda i, j: (i, j, 0, 0))
    kblk = pl.BlockSpec((1, 1, n_k, hd), lambda i, j: (i, j, 0, 0))
    return pl.pallas_call(
        functools.partial(_attn_kernel, n_rope=n_rope, tq=tq),
        grid=(b, n_kv),
        in_specs=[qblk, kblk, kblk, _resident(cos.shape), _resident(sin.shape),
                  _resident(q_gain.shape), _resident(k_gain.shape)],
        out_specs=qblk,
        out_shape=jax.ShapeDtypeStruct(q4.shape, F32),
        scratch_shapes=[pltpu.VMEM((n_k, hd), BF16), pltpu.VMEM((n_k, hd), BF16)],
        compiler_params=_cparams("arbitrary", "arbitrary"),
        name="attention",
    )(q4, k4, v4, cos, sin, q_gain, k_gain)


def _merge_kernel(re_ref, mx_ref, at_ref, gate_ref, x_ref, mod_ref, fw_ref, pw_ref, aw_ref, wo_ref, o_ref):
    d = x_ref.shape[1]
    yf = jnp.dot(re_ref[...].astype(BF16), fw_ref[...], preferred_element_type=F32)
    yp = jnp.dot(mx_ref[...].astype(BF16), pw_ref[...], preferred_element_type=F32)
    ya = jnp.dot(at_ref[...].astype(BF16), aw_ref[...], preferred_element_type=F32)
    z = gate_ref[:, 0:d] * yf + gate_ref[:, d:2 * d] * yp + gate_ref[:, 2 * d:3 * d] * ya
    y = jnp.dot(z.astype(BF16), wo_ref[...], preferred_element_type=F32)
    o_ref[...] = x_ref[...] + mod_ref[0, 2:3, :] * y


def _merge(re, mixed, attn, gates, x, mods, fw, pw, aw, wo, tile_row, n_tokens):
    d = x.shape[1]
    tm = TOKEN_TILE
    row = lambda i: (i, 0)
    ins = [re, mixed, attn, gates, x]
    return pl.pallas_call(
        _merge_kernel,
        grid=(n_tokens // tm,),
        in_specs=[pl.BlockSpec((tm, a.shape[1]), row) for a in ins]
        + [pl.BlockSpec((1,) + mods.shape[1:], lambda i: (tile_row(i), 0, 0))]
        + [_resident(w.shape) for w in (fw, pw, aw, wo)],
        out_specs=pl.BlockSpec((tm, d), row),
        out_shape=jax.ShapeDtypeStruct((n_tokens, d), F32),
        compiler_params=_cparams("arbitrary"),
        name="merge",
    )(*ins, mods, fw, pw, aw, wo)


def _topk_rows(s, k):
    n = s.shape[0]
    rows = lax.broadcasted_iota(jnp.int32, s.shape, 0)
    vals, ids = [], []
    for _ in range(k):
        m = jnp.max(s, axis=0, keepdims=True)
        r = jnp.min(jnp.where(s == m, rows, n), axis=0, keepdims=True)
        vals.append(m)
        ids.append(r)
        s = jnp.where(rows == r, NEG_INF, s)
    return jnp.concatenate(vals, axis=0), jnp.concatenate(ids, axis=0)


def _pair_candidates(s1, i1, s2, i2):
    k = PEER_TOPK
    row8 = lax.broadcasted_iota(jnp.int32, (SUBLANES, 1), 0)
    cand, expert = [s1[0:1] + s2], [i1[0:1] * PEER_KEYS + i2]
    for a in range(1, SUBLANES):
        nb = k // (a + 1)
        c = s1[a:a + 1] + s2[0:SUBLANES]
        cand.append(jnp.where(row8 < nb, c, NEG_INF))
        expert.append(i1[a:a + 1] * PEER_KEYS + i2[0:SUBLANES])
    cand.append(s1[SUBLANES:k] + s2[0:1])
    expert.append(i1[SUBLANES:k] * PEER_KEYS + i2[0:1])
    return jnp.concatenate(cand, axis=0), jnp.concatenate(expert, axis=0)


def _peer_q_kernel(x_ref, mod_ref, g_ref, wq_ref, keys_ref, h_ref, e_ref, gw_ref, q_scr):
    kd = keys_ref.shape[2]
    h = _rms(x_ref[...], g_ref[...]) * (1.0 + mod_ref[0, 4:5, :]) + mod_ref[0, 3:4, :]
    h_ref[...] = h
    hb = h.astype(BF16)
    for j in range(2 * PEER_HEADS):
        q_scr[j] = jnp.dot(hb, wq_ref[:, j * kd:(j + 1) * kd], preferred_element_type=F32).astype(BF16)

    def head(hh, carry):
        tops = []
        for p in range(2):
            s = lax.dot_general(keys_ref[2 * hh + p], q_scr[2 * hh + p], (((1,), (1,)), ((), ())),
                                preferred_element_type=F32)
            tops.append(_topk_rows(s, PEER_TOPK))
        cand, expert = _pair_candidates(tops[0][0], tops[0][1], tops[1][0], tops[1][1])
        n = cand.shape[0]
        rows = lax.broadcasted_iota(jnp.int32, cand.shape, 0)
        top, eid = [], []
        for _ in range(PEER_TOPK):
            m = jnp.max(cand, axis=0, keepdims=True)
            r = jnp.min(jnp.where(cand == m, rows, n), axis=0, keepdims=True)
            sel = rows == r
            top.append(m)
            eid.append(jnp.max(jnp.where(sel, expert, -1), axis=0, keepdims=True))
            cand = jnp.where(sel, NEG_INF, cand)
        top = jnp.concatenate(top, axis=0)
        ex = jnp.exp(top - top[0:1])
        r0 = pl.multiple_of(hh * PEER_TOPK, PEER_TOPK)
        gw_ref[pl.ds(r0, PEER_TOPK), :] = ex / jnp.sum(ex, axis=0, keepdims=True)
        e_ref[pl.ds(r0, PEER_TOPK), :] = jnp.concatenate(eid, axis=0)
        return carry

    lax.fori_loop(0, PEER_HEADS, head, 0)


def _peer_query(x, mods, gain, wq_b, keys_b, tile_row, n_tokens):
    d = x.shape[1]
    tm = TOKEN_TILE
    n_sel = PEER_HEADS * PEER_TOPK
    kd = keys_b.shape[2]
    return pl.pallas_call(
        _peer_q_kernel,
        grid=(n_tokens // tm,),
        in_specs=[pl.BlockSpec((tm, d), lambda i: (i, 0)),
                  pl.BlockSpec((1,) + mods.shape[1:], lambda i: (tile_row(i), 0, 0)),
                  _resident(gain.shape), _resident(wq_b.shape), _resident(keys_b.shape)],
        out_specs=[pl.BlockSpec((tm, d), lambda i: (i, 0)),
                   pl.BlockSpec((n_sel, tm), lambda i: (0, i)),
                   pl.BlockSpec((n_sel, tm), lambda i: (0, i))],
        out_shape=[jax.ShapeDtypeStruct((n_tokens, d), F32),
                   jax.ShapeDtypeStruct((n_sel, n_tokens), jnp.int32),
                   jax.ShapeDtypeStruct((n_sel, n_tokens), F32)],
        scratch_shapes=[pltpu.VMEM((2 * PEER_HEADS, tm, kd), BF16)],
        compiler_params=_cparams("arbitrary"),
        name="peer_query",
    )(x, mods, gain, wq_b, keys_b)


FOLD_SHIFTS = (1, 2, 4)


def _fold_pairs(vs, shift):
    sub = lax.broadcasted_iota(jnp.int32, vs[0].shape, 0)
    keep = (sub % (2 * shift)) < shift
    return [jnp.where(keep, vs[i], vs[i + 1]) + pltpu.roll(jnp.where(keep, vs[i + 1], vs[i]), shift, axis=0)
            for i in range(0, len(vs), 2)]


def _trace_fold_rows():
    vs = [[i] * SUBLANES for i in range(SUBLANES)]
    for shift in FOLD_SHIFTS:
        vs = [[vs[i][r] if (r % (2 * shift)) < shift else vs[i + 1][r] for r in range(SUBLANES)]
              for i in range(0, len(vs), 2)]
    return vs[0]


_SLOT_PERM = _trace_fold_rows()


def _pack_rows(tab):
    n = tab.shape[0]
    t = tab.astype(BF16).reshape(n, SUBLANES // 2, 2, LANES).transpose(0, 1, 3, 2)
    return lax.bitcast_convert_type(t, jnp.uint32).reshape(n * (SUBLANES // 2), LANES)


def _gather_row(tab_ref, row4):
    words = tab_ref[pl.ds(pl.multiple_of(row4, SUBLANES // 2), SUBLANES // 2), :]
    return pltpu.bitcast(words, BF16).astype(F32)


def _split3(v):
    p0 = v.astype(BF16)
    r = v - p0.astype(F32)
    p1 = r.astype(BF16)
    p2 = (r - p1.astype(F32)).astype(BF16)
    return p0, p1, p2


def _peer_u_kernel(*refs):
    idx_refs = refs[:PEER_IDX_PARTS]
    x_ref, tab_ref, g_ref, fold_ref, o_ref, a_scr, at_scr = refs[PEER_IDX_PARTS:PEER_IDX_PARTS + 7]
    stages = refs[PEER_IDX_PARTS + 7:]
    tg, n_sel = g_ref.shape
    group = len(stages)
    per_part = n_sel // PEER_IDX_PARTS
    mxu_parts = PEER_U_MXU_PARTS
    vpu_parts = PEER_IDX_PARTS - mxu_parts
    n_mxu = mxu_parts * per_part
    n_vpu = n_sel - n_mxu
    half = SUBLANES // 2
    n_rows = n_mxu * SUBLANES
    rows = lax.broadcasted_iota(jnp.int32, (SUBLANES, n_rows), 0)
    cols = lax.broadcasted_iota(jnp.int32, (SUBLANES, n_rows), 1)
    same_row = rows == (cols & (SUBLANES - 1))
    lane = lax.broadcasted_iota(jnp.int32, (n_vpu, LANES), 1)
    at_scr[...] = jnp.zeros_like(at_scr)

    def stage_tiles(t, s_ref):
        part_rows = [r.at[pl.ds(t * per_part, per_part)] for r in idx_refs[:mxu_parts]]
        for j in range(per_part):
            for q in range(mxu_parts):
                k = q * per_part + j
                s_ref[k * half:(k + 1) * half, :] = tab_ref[pl.ds(pl.multiple_of(part_rows[q][j], half), half), :]

    def row_dots(t, s_ref):
        tiles = pltpu.bitcast(s_ref[...], BF16)
        xs = jnp.concatenate(_split3(x_ref[t]), axis=0)
        xs = jnp.concatenate([xs, jnp.zeros((SUBLANES, LANES), BF16)], axis=0)
        dots = lax.dot_general(xs, tiles, (((1,), (1,)), ((), ())), preferred_element_type=F32)
        per_piece = [jnp.where(same_row, dots[p * SUBLANES:(p + 1) * SUBLANES], 0.0) for p in range(3)]
        return (per_piece[0] + per_piece[1]) + per_piece[2]

    def lane_partials(t):
        xt = x_ref[t]
        part_rows = [r.at[pl.ds(t * per_part, per_part)] for r in idx_refs[mxu_parts:]]
        prods = [None] * n_vpu
        for j in range(per_part):
            for q in range(vpu_parts):
                f, s = divmod(q * per_part + j, SUBLANES)
                prods[f * SUBLANES + _SLOT_PERM[s]] = _gather_row(tab_ref, part_rows[q][j]) * xt
        for shift in FOLD_SHIFTS:
            prods = _fold_pairs(prods, shift)
        return jnp.concatenate(prods, axis=0)

    def step(i, carry):
        t0 = pl.multiple_of(i * group, group)
        for j in range(group):
            stage_tiles(t0 + j, stages[j])
        at = at_scr[0:n_vpu, :]
        for j in range(group):
            col = jnp.sum(lane_partials(t0 + j), axis=1, keepdims=True)
            at = jnp.where(lane == t0 + j, col, at)
        at_scr[0:n_vpu, :] = at
        parts = [None] * group
        for j in range(group):
            parts[_SLOT_PERM[j]] = row_dots(t0 + j, stages[j])
        for shift in FOLD_SHIFTS:
            parts = _fold_pairs(parts, shift)
        per_token = parts[0]
        pieces = jnp.concatenate(_split3(per_token) + (jnp.zeros((SUBLANES, n_rows), BF16),), axis=0)
        a = jnp.dot(pieces, fold_ref[...], preferred_element_type=F32)
        a_scr[pl.ds(t0, group), :] = (a[0:SUBLANES] + a[SUBLANES:2 * SUBLANES]) + a[2 * SUBLANES:3 * SUBLANES]
        return carry

    lax.fori_loop(0, tg // group, step, 0)
    a_all = jnp.concatenate([a_scr[...], at_scr[...].T[:, 0:n_vpu]], axis=1)
    o_ref[...] = g_ref[...] * jax.nn.gelu(a_all)


def _peer_u(idx_parts, h3, tab, g_tm, fold):
    t, n_sel = g_tm.shape
    tg = PEER_U_TILE
    assert tg == LANES
    n_mxu = PEER_U_MXU_PARTS * n_sel // PEER_IDX_PARTS
    per_tile = tg * n_sel // PEER_IDX_PARTS
    stage = pltpu.VMEM((n_mxu * SUBLANES // 2, LANES), jnp.uint32)
    return pl.pallas_call(
        _peer_u_kernel,
        grid=(t // tg,),
        in_specs=[pl.BlockSpec((per_tile,), lambda i: (i,), memory_space=pltpu.SMEM) for _ in idx_parts]
        + [pl.BlockSpec((tg,) + h3.shape[1:], lambda i: (i, 0, 0)),
           pl.BlockSpec(tab.shape, lambda i: (0, 0), pipeline_mode=pl.Buffered(1)),
           pl.BlockSpec((tg, n_sel), lambda i: (i, 0)),
           _resident(fold.shape)],
        out_specs=pl.BlockSpec((tg, n_sel), lambda i: (i, 0)),
        out_shape=jax.ShapeDtypeStruct((t, n_sel), F32),
        scratch_shapes=[pltpu.VMEM((tg, n_mxu), F32), pltpu.VMEM((LANES, LANES), F32)] + [stage] * SUBLANES,
        compiler_params=_cparams("arbitrary"),
        name="peer_u",
    )(*idx_parts, h3, tab, g_tm, fold)


def _split_index(idx_tm, parts):
    t, n = idx_tm.shape
    split = idx_tm.reshape(t, parts, n // parts)
    return [split[:, q, :].reshape(-1) for q in range(parts)]


def _peer_v_kernel(*refs):
    idx_refs = refs[:PEER_IDX_PARTS]
    w_ref, tab_ref, o_ref, wb_scr = refs[PEER_IDX_PARTS:]
    n_sel, tg = w_ref.shape
    per_part = n_sel // PEER_IDX_PARTS
    n_acc = 4
    for t in range(tg):
        wb_scr[t] = jnp.broadcast_to(w_ref[:, t:t + 1], (n_sel, LANES))

    def token(t, carry):
        rows = [r.at[pl.ds(t * per_part, per_part)] for r in idx_refs]
        acc = [None] * n_acc
        for j in range(per_part):
            for q in range(PEER_IDX_PARTS):
                k = q * per_part + j
                term = _gather_row(tab_ref, rows[q][j]) * wb_scr[t, k:k + 1, :]
                a = k % n_acc
                acc[a] = term if acc[a] is None else acc[a] + term
        o_ref[t] = (acc[0] + acc[1]) + (acc[2] + acc[3])
        return carry

    lax.fori_loop(0, tg, token, 0)


def _peer_v(idx_parts, w_km, tab):
    n_sel, t = w_km.shape
    tg = PEER_V_TILE
    per_tile = tg * n_sel // PEER_IDX_PARTS
    return pl.pallas_call(
        _peer_v_kernel,
        grid=(t // tg,),
        in_specs=[pl.BlockSpec((per_tile,), lambda i: (i,), memory_space=pltpu.SMEM) for _ in idx_parts]
        + [pl.BlockSpec((n_sel, tg), lambda i: (0, i)),
           pl.BlockSpec(tab.shape, lambda i: (0, 0), pipeline_mode=pl.Buffered(1))],
        out_specs=pl.BlockSpec((tg, SUBLANES, LANES), lambda i: (i, 0, 0)),
        out_shape=jax.ShapeDtypeStruct((t, SUBLANES, LANES), F32),
        scratch_shapes=[pltpu.VMEM((tg, n_sel, LANES), F32)],
        compiler_params=_cparams("arbitrary"),
        name="peer_v",
    )(*idx_parts, w_km, tab)


def _residual_kernel(x_ref, y_ref, mod_ref, o_ref):
    o_ref[...] = x_ref[...] + mod_ref[0, 5:6, :] * y_ref[...]


def _residual(x, y, mods, tile_row, n_tokens):
    d = x.shape[1]
    tm = TOKEN_TILE
    row = lambda i: (i, 0)
    return pl.pallas_call(
        _residual_kernel,
        grid=(n_tokens // tm,),
        in_specs=[pl.BlockSpec((tm, d), row), pl.BlockSpec((tm, d), row),
                  pl.BlockSpec((1,) + mods.shape[1:], lambda i: (tile_row(i), 0, 0))],
        out_specs=pl.BlockSpec((tm, d), row),
        out_shape=jax.ShapeDtypeStruct((n_tokens, d), F32),
        compiler_params=_cparams("arbitrary"),
        name="residual",
    )(x, y, mods)


def _dft_tables(n):
    j = jnp.arange(n, dtype=jnp.int32)
    ang = (2.0 * math.pi / n) * ((j[:, None] * j[None, :]) % n).astype(F32)
    return jnp.cos(ang), jnp.sin(ang)


def _rope_tables(n):
    f = HEAD_DIM // 4
    rows = n // GRID_W
    row = jnp.repeat(jnp.arange(rows), GRID_W).astype(F32)
    col = jnp.tile(jnp.arange(GRID_W), rows).astype(F32)
    inv = ROPE_THETA ** (-jnp.arange(f, dtype=F32) / f)
    ar, ac = row[:, None] * inv, col[:, None] * inv
    cos = jnp.concatenate([jnp.cos(ar), jnp.cos(ar), jnp.cos(ac), jnp.cos(ac)], axis=1)
    sin = jnp.concatenate([-jnp.sin(ar), jnp.sin(ar), -jnp.sin(ac), jnp.sin(ac)], axis=1)
    return cos, sin


def _block_diag(blocks):
    g, r, c = blocks.shape
    out = jnp.zeros((g * r, g * c), blocks.dtype)
    for i in range(g):
        out = out.at[i * r:(i + 1) * r, i * c:(i + 1) * c].set(blocks[i])
    return out


def _seq_dft(fg, n_batch, n_seq, cs):
    w2 = fg.shape[1]
    w = w2 // 2
    g = fg.reshape(n_batch, n_seq, 2, w).transpose(2, 1, 0, 3).reshape(2 * n_seq, n_batch * w)
    tm = min(512, n_seq)
    re = _matmul(cs, g, tm, min(512, n_batch * w))
    return re.reshape(n_seq, n_batch, w).transpose(1, 0, 2).reshape(n_batch * n_seq, w)


def kernel(x, c, ctx, c_ctx, ada_w, ada_b, norm_mix, w_in, fourier_w, pool_w, pool_scale, pool_proj,
           q_norm, k_norm, attn_proj, w_out, norm_ffn, peer_wq, peer_keys, peer_u, peer_v):
    nb, n_lat, d = x.shape
    n_ctx = ctx.shape[1]
    depth = ada_w.shape[0]
    fw = fourier_w.shape[1]
    pw = pool_proj.shape[1]
    qw = attn_proj.shape[1]
    n_heads = qw // HEAD_DIM
    n_kv = n_heads // GQA_GROUP
    kw = n_kv * HEAD_DIM
    widths = (fw, pw, qw, kw)
    t_lat, t_ctx = nb * n_lat, nb * n_ctx
    t_all = t_lat + t_ctx
    n_exp = peer_u.shape[1]
    n_sel = PEER_HEADS * PEER_TOPK
    assert n_lat % TOKEN_TILE == 0 and n_ctx % TOKEN_TILE == 0 and t_all % PEER_U_TILE == 0
    assert d == SUBLANES * LANES

    lat_tiles = t_lat // TOKEN_TILE
    per_sample = n_lat // TOKEN_TILE
    tile_row = lambda i: jnp.where(i < lat_tiles, i // per_sample, nb)

    rows = -(-(nb + 1) // SUBLANES) * SUBLANES
    cvec = jnp.zeros((rows, d), F32).at[:nb].set(c).at[nb].set(c_ctx)
    mods_all = _adaln(cvec, ada_w, ada_b).reshape(depth, rows, 6, d)

    gd = fw // FOURIER_GROUPS
    cc, sc = _dft_tables(gd)
    norm = 1.0 / math.sqrt(gd)
    eye = jnp.eye(FOURIER_GROUPS, dtype=F32)
    dft_c = jnp.concatenate([jnp.kron(eye, cc), jnp.kron(eye, sc)], axis=1) * norm
    dft_c = dft_c.astype(BF16)

    def seq_tables(n):
        cl, sl = _dft_tables(n)
        return (jnp.concatenate([cl, -sl], axis=1) * (1.0 / math.sqrt(n))).astype(BF16)

    cs_lat, cs_ctx = seq_tables(n_lat), seq_tables(n_ctx)
    cos, sin = _rope_tables(n_lat)

    n_mxu = PEER_U_MXU_PARTS * n_sel // PEER_IDX_PARTS
    fold = jnp.repeat(jnp.eye(n_mxu, dtype=BF16), SUBLANES, axis=0)

    stream = jnp.concatenate([x.reshape(t_lat, d), ctx.reshape(t_ctx, d)], axis=0)

    for l in range(depth):
        last = l == depth - 1
        mods = mods_all[l]
        w_in_b = w_in[l].astype(BF16)
        fg, p, q, k, v, gates = _in_proj(stream, mods, norm_mix[l][None, :], w_in_b, dft_c, tile_row, widths)

        re_lat = _seq_dft(fg[:t_lat], nb, n_lat, cs_lat)
        w_bd = _block_diag(pool_w[l]).astype(BF16)
        scale = pool_scale[l][None, :]
        mx_lat = _pool(p[:t_lat].reshape(nb, n_lat, pw), w_bd, scale).reshape(t_lat, pw)
        def heads(a, n, h):
            return a.reshape(nb, n, h, HEAD_DIM).transpose(0, 2, 1, 3)
        q_lat, q_ctx = heads(q[:t_lat], n_lat, n_heads), heads(q[t_lat:], n_ctx, n_heads)
        k_lat, k_ctx = heads(k[:t_lat], n_lat, n_kv), heads(k[t_lat:], n_ctx, n_kv)
        v_lat, v_ctx = heads(v[:t_lat], n_lat, n_kv), heads(v[t_lat:], n_ctx, n_kv)
        qg, kg = q_norm[l][None, :], k_norm[l][None, :]
        at_lat = _attention(q_lat, jnp.concatenate([k_lat, k_ctx], axis=2),
                            jnp.concatenate([v_lat, v_ctx], axis=2), cos, sin, qg, kg, n_lat)
        at_lat = at_lat.transpose(0, 2, 1, 3).reshape(t_lat, qw)

        if last:
            re, mixed, attn, n_tok = re_lat, mx_lat, at_lat, t_lat
        else:
            re_ctx = _seq_dft(fg[t_lat:], nb, n_ctx, cs_ctx)
            mx_ctx = _pool(p[t_lat:].reshape(nb, n_ctx, pw), w_bd, scale).reshape(t_ctx, pw)
            at_ctx = _attention(q_ctx, k_ctx, v_ctx, cos, sin, qg, kg, 0)
            at_ctx = at_ctx.transpose(0, 2, 1, 3).reshape(t_ctx, qw)
            re = jnp.concatenate([re_lat, re_ctx], axis=0)
            mixed = jnp.concatenate([mx_lat, mx_ctx], axis=0)
            attn = jnp.concatenate([at_lat, at_ctx], axis=0)
            n_tok = t_all

        x1 = _merge(re, mixed, attn, gates, stream, mods,
                    fourier_w[l].astype(BF16), pool_proj[l].astype(BF16),
                    attn_proj[l].astype(BF16), w_out[l].astype(BF16), tile_row, n_tok)

        keys_b = peer_keys[l].reshape(2 * PEER_HEADS, PEER_KEYS, -1).astype(BF16)
        h2, e_t, g_t = _peer_query(x1, mods, norm_ffn[l][None, :], peer_wq[l].astype(BF16), keys_b,
                                   tile_row, n_tok)
        idx_parts = _split_index(e_t.T * (SUBLANES // 2), PEER_IDX_PARTS)
        w_tm = _peer_u(idx_parts, h2.reshape(n_tok, SUBLANES, LANES), _pack_rows(peer_u[l]), g_t.T, fold)
        y = _peer_v(idx_parts, w_tm.T, _pack_rows(peer_v[l])).reshape(n_tok, d)
        stream = _residual(x1, y, mods, tile_row, n_tok)

    return stream[:t_lat].reshape(nb, n_lat, d)
```

```python
import functools
import math

import jax
import jax.numpy as jnp
from jax import lax
from jax.experimental import pallas as pl
from jax.experimental.pallas import tpu as pltpu

EPS = 1e-6
GRID_W = 64
HEAD_DIM = 64
GQA_GROUP = 4
ROPE_THETA = 10000.0
POOL_WINDOWS = (2, 4, 8, 16)
FOURIER_GROUPS = 4
PEER_HEADS = 8
PEER_KEYS = 128
PEER_TOPK = 16

LANES = 128
SUBLANES = 8
VMEM_LIMIT = 56 * 1024 * 1024

TOKEN_TILE = 256
PEER_Q_TILE = 256
ATTN_Q_TILE = 256
PEER_U_TILE = 128
PEER_V_TILE = 128
PEER_IDX_PARTS = 8
PEER_U_MXU_PARTS = 3

BF16 = jnp.bfloat16
F32 = jnp.float32
NEG_INF = float("-inf")


def _cparams(*sem):
    return pltpu.CompilerParams(dimension_semantics=sem, vmem_limit_bytes=VMEM_LIMIT)


def _rms(x, gain):
    return x * lax.rsqrt(jnp.mean(x * x, axis=-1, keepdims=True) + EPS) * gain


def _resident(shape):
    zeros = (0,) * len(shape)
    return pl.BlockSpec(shape, lambda *_: zeros)


def _adaln_kernel(c_ref, w_ref, b_ref, o_ref):
    c = c_ref[...]
    s = c * jax.nn.sigmoid(c)
    o_ref[0] = jnp.dot(s.astype(BF16), w_ref[0].astype(BF16), preferred_element_type=F32) + b_ref[0]


def _adaln(cvec, ada_w, ada_b):
    depth, d, six_d = ada_w.shape
    rows = cvec.shape[0]
    tn = 1024
    return pl.pallas_call(
        _adaln_kernel,
        grid=(depth, six_d // tn),
        in_specs=[pl.BlockSpec((rows, d), lambda l, j: (0, 0)),
                  pl.BlockSpec((1, d, tn), lambda l, j: (l, 0, j)),
                  pl.BlockSpec((1, 1, tn), lambda l, j: (l, 0, j))],
        out_specs=pl.BlockSpec((1, rows, tn), lambda l, j: (l, 0, j)),
        out_shape=jax.ShapeDtypeStruct((depth, rows, six_d), F32),
        compiler_params=_cparams("arbitrary", "arbitrary"),
        name="adaln",
    )(cvec, ada_w, ada_b.reshape(depth, 1, six_d))


def _proj_kernel(x_ref, mod_ref, g_ref, w_ref, dft_ref,
                 fg_ref, p_ref, q_ref, k_ref, v_ref, gate_ref, *, widths):
    fw, pw, qw, kw = widths
    x = x_ref[...]
    h = _rms(x, g_ref[...]) * (1.0 + mod_ref[0, 1:2, :]) + mod_ref[0, 0:1, :]
    hb = h.astype(BF16)

    def mm(c0, c1):
        return jnp.dot(hb, w_ref[:, c0:c1], preferred_element_type=F32)

    c = 0
    f = mm(c, c + fw); c += fw
    fg_ref[...] = jnp.dot(f.astype(BF16), dft_ref[...], preferred_element_type=F32).astype(BF16)
    p_ref[...] = mm(c, c + pw); c += pw
    q_ref[...] = mm(c, c + qw); c += qw
    k_ref[...] = mm(c, c + kw); c += kw
    v_ref[...] = mm(c, c + kw); c += kw
    n_gate = gate_ref.shape[1]
    step = 1024
    for j in range(0, n_gate, step):
        gate_ref[:, j:j + step] = jax.nn.sigmoid(mm(c + j, c + j + step))


def _in_proj(x, mods, gain, w_in_b, dft_c, tile_row, widths):
    t, d = x.shape
    fw, pw, qw, kw = widths
    n_gate = w_in_b.shape[1] - (fw + pw + qw + 2 * kw)
    tm = TOKEN_TILE
    row = lambda i: (i, 0)
    outs = [(2 * fw, BF16), (pw, F32), (qw, F32), (kw, F32), (kw, F32), (n_gate, F32)]
    return pl.pallas_call(
        functools.partial(_proj_kernel, widths=widths),
        grid=(t // tm,),
        in_specs=[pl.BlockSpec((tm, d), row),
                  pl.BlockSpec((1,) + mods.shape[1:], lambda i: (tile_row(i), 0, 0)),
                  _resident(gain.shape), _resident(w_in_b.shape), _resident(dft_c.shape)],
        out_specs=[pl.BlockSpec((tm, w), row) for w, _ in outs],
        out_shape=[jax.ShapeDtypeStruct((t, w), dt) for w, dt in outs],
        compiler_params=_cparams("arbitrary"),
        name="in_proj",
    )(x, mods, gain, w_in_b, dft_c)


def _mm_kernel(a_ref, b_ref, o_ref):
    o_ref[...] = jnp.dot(a_ref[...], b_ref[...], preferred_element_type=F32)


def _matmul(a, b, tm, tn):
    m, k = a.shape
    n = b.shape[1]
    return pl.pallas_call(
        _mm_kernel,
        grid=(m // tm, n // tn),
        in_specs=[pl.BlockSpec((tm, k), lambda i, j: (i, 0)),
                  pl.BlockSpec((k, tn), lambda i, j: (0, j))],
        out_specs=pl.BlockSpec((tm, tn), lambda i, j: (i, j)),
        out_shape=jax.ShapeDtypeStruct((m, n), F32),
        compiler_params=_cparams("arbitrary", "arbitrary"),
        name="seq_dft",
    )(a, b)


def _pool_kernel(p_ref, w_ref, s_ref, o_ref):
    x = p_ref[0]
    n, width = x.shape
    gdim = width // len(POOL_WINDOWS)
    t = lax.broadcasted_iota(jnp.int32, (n, 1), 0)
    col = lax.broadcasted_iota(jnp.int32, (1, width), 1)
    half = jnp.zeros((1, width), jnp.int32)
    for gi, win in enumerate(POOL_WINDOWS):
        half = jnp.where((col >= gi * gdim) & (col < (gi + 1) * gdim), win // 2, half)
    acc = jnp.zeros_like(x)
    max_half = max(POOL_WINDOWS) // 2
    for d in range(-max_half, max_half):
        shifted = x if d == 0 else pltpu.roll(x, (-d) % n, axis=0)
        row_ok = (t + d >= 0) & (t + d < n)
        col_ok = (half >= -d) if d < 0 else (half > d)
        acc = acc + jnp.where(row_ok, jnp.where(col_ok, shifted, 0.0), 0.0)
    cnt = (jnp.minimum(t + half, n) - jnp.maximum(t - half, 0)).astype(F32)
    pooled = acc / cnt - x
    o_ref[0] = jnp.dot(pooled.astype(BF16), w_ref[...], preferred_element_type=F32) * s_ref[...]


def _pool(p3, w_bd, scale):
    b, n, width = p3.shape
    blk = pl.BlockSpec((1, n, width), lambda i: (i, 0, 0))
    return pl.pallas_call(
        _pool_kernel,
        grid=(b,),
        in_specs=[blk, _resident(w_bd.shape), _resident(scale.shape)],
        out_specs=blk,
        out_shape=jax.ShapeDtypeStruct(p3.shape, F32),
        compiler_params=_cparams("arbitrary"),
        name="pool",
    )(p3, w_bd, scale)


def _swap_rope_halves(x):
    q = HEAD_DIM // 4
    return jnp.concatenate([x[:, q:2 * q], x[:, 0:q], x[:, 3 * q:4 * q], x[:, 2 * q:3 * q]], axis=1)


def _attn_kernel(q_ref, k_ref, v_ref, cos_ref, sin_ref, qn_ref, kn_ref, o_ref, kb_ref, vb_ref,
                 *, n_rope, tq):
    n_q = q_ref.shape[2]
    n_k = k_ref.shape[2]
    kn = _rms(k_ref[0, 0], kn_ref[...])
    if n_rope:
        k_lat = kn[:n_rope]
        kb_ref[0:n_rope, :] = (k_lat * cos_ref[...] + _swap_rope_halves(k_lat) * sin_ref[...]).astype(BF16)
        if n_k > n_rope:
            kb_ref[n_rope:n_k, :] = kn[n_rope:].astype(BF16)
    else:
        kb_ref[...] = kn.astype(BF16)
    vb_ref[...] = v_ref[0, 0].astype(BF16)
    n_tiles = n_q // tq

    def block(i, carry):
        g = i // n_tiles
        r0 = pl.multiple_of((i % n_tiles) * tq, tq)
        qn = _rms(q_ref[0, g, pl.ds(r0, tq), :], qn_ref[...])
        if n_rope:
            qn = qn * cos_ref[pl.ds(r0, tq), :] + _swap_rope_halves(qn) * sin_ref[pl.ds(r0, tq), :]
        qb = (qn * (HEAD_DIM ** -0.5)).astype(BF16)
        s = lax.dot_general(qb, kb_ref[...], (((1,), (1,)), ((), ())), preferred_element_type=F32)
        m = jnp.max(s, axis=-1, keepdims=True)
        p = jnp.exp(s - m)
        l = jnp.sum(p, axis=-1, keepdims=True)
        o = jnp.dot(p.astype(BF16), vb_ref[...], preferred_element_type=F32)
        o_ref[0, g, pl.ds(r0, tq), :] = o / l
        return carry

    lax.fori_loop(0, GQA_GROUP * n_tiles, block, 0)


def _attention(q4, k4, v4, cos, sin, q_gain, k_gain, n_rope):
    b, n_heads, n_q, hd = q4.shape
    n_kv = k4.shape[1]
    n_k = k4.shape[2]
    tq = min(ATTN_Q_TILE, n_q)
    qblk = pl.BlockSpec((1, GQA_GROUP, n_q, hd), lambda i, j: (i, j, 0, 0))
    kblk = pl.BlockSpec((1, 1, n_k, hd), lambda i, j: (i, j, 0, 0))
    return pl.pallas_call(
        functools.partial(_attn_kernel, n_rope=n_rope, tq=tq),
        grid=(b, n_kv),
        in_specs=[qblk, kblk, kblk, _resident(cos.shape), _resident(sin.shape),
                  _resident(q_gain.shape), _resident(k_gain.shape)],
        out_specs=qblk,
        out_shape=jax.ShapeDtypeStruct(q4.shape, F32),
        scratch_shapes=[pltpu.VMEM((n_k, hd), BF16), pltpu.VMEM((n_k, hd), BF16)],
        compiler_params=_cparams("arbitrary", "arbitrary"),
        name="attention",
    )(q4, k4, v4, cos, sin, q_gain, k_gain)


def _merge_kernel(re_ref, mx_ref, at_ref, gate_ref, x_ref, mod_ref, fw_ref, pw_ref, aw_ref, wo_ref, o_ref):
    d = x_ref.shape[1]
    yf = jnp.dot(re_ref[...].astype(BF16), fw_ref[...], preferred_element_type=F32)
    yp = jnp.dot(mx_ref[...].astype(BF16), pw_ref[...], preferred_element_type=F32)
    ya = jnp.dot(at_ref[...].astype(BF16), aw_ref[...], preferred_element_type=F32)
    z = gate_ref[:, 0:d] * yf + gate_ref[:, d:2 * d] * yp + gate_ref[:, 2 * d:3 * d] * ya
    y = jnp.dot(z.astype(BF16), wo_ref[...], preferred_element_type=F32)
    o_ref[...] = x_ref[...] + mod_ref[0, 2:3, :] * y


def _merge(re, mixed, attn, gates, x, mods, fw, pw, aw, wo, tile_row, n_tokens):
    d = x.shape[1]
    tm = TOKEN_TILE
    row = lambda i: (i, 0)
    ins = [re, mixed, attn, gates, x]
    return pl.pallas_call(
        _merge_kernel,
        grid=(n_tokens // tm,),
        in_specs=[pl.BlockSpec((tm, a.shape[1]), row) for a in ins]
        + [pl.BlockSpec((1,) + mods.shape[1:], lambda i: (tile_row(i), 0, 0))]
        + [_resident(w.shape) for w in (fw, pw, aw, wo)],
        out_specs=pl.BlockSpec((tm, d), row),
        out_shape=jax.ShapeDtypeStruct((n_tokens, d), F32),
        compiler_params=_cparams("arbitrary"),
        name="merge",
    )(*ins, mods, fw, pw, aw, wo)


def _topk_rows(s, k):
    n = s.shape[0]
    rows = lax.broadcasted_iota(jnp.int32, s.shape, 0)
    vals, ids = [], []
    for _ in range(k):
        m = jnp.max(s, axis=0, keepdims=True)
        r = jnp.min(jnp.where(s == m, rows, n), axis=0, keepdims=True)
        vals.append(m)
        ids.append(r)
        s = jnp.where(rows == r, NEG_INF, s)
    return jnp.concatenate(vals, axis=0), jnp.concatenate(ids, axis=0)


def _pair_candidates(s1, i1, s2, i2):
    k = PEER_TOPK
    row8 = lax.broadcasted_iota(jnp.int32, (SUBLANES, 1), 0)
    cand, expert = [s1[0:1] + s2], [i1[0:1] * PEER_KEYS + i2]
    for a in range(1, SUBLANES):
        nb = k // (a + 1)
        c = s1[a:a + 1] + s2[0:SUBLANES]
        cand.append(jnp.where(row8 < nb, c, NEG_INF))
        expert.append(i1[a:a + 1] * PEER_KEYS + i2[0:SUBLANES])
    cand.append(s1[SUBLANES:k] + s2[0:1])
    expert.append(i1[SUBLANES:k] * PEER_KEYS + i2[0:1])
    return jnp.concatenate(cand, axis=0), jnp.concatenate(expert, axis=0)


def _peer_q_kernel(x_ref, mod_ref, g_ref, wq_ref, keys_ref, h_ref, e_ref, gw_ref, q_scr):
    kd = keys_ref.shape[2]
    h = _rms(x_ref[...], g_ref[...]) * (1.0 + mod_ref[0, 4:5, :]) + mod_ref[0, 3:4, :]
    h_ref[...] = h
    hb = h.astype(BF16)
    for j in range(2 * PEER_HEADS):
        q_scr[j] = jnp.dot(hb, wq_ref[:, j * kd:(j + 1) * kd], preferred_element_type=F32).astype(BF16)

    def head(hh, carry):
        tops = []
        for p in range(2):
            s = lax.dot_general(keys_ref[2 * hh + p], q_scr[2 * hh + p], (((1,), (1,)), ((), ())),
                                preferred_element_type=F32)
            tops.append(_topk_rows(s, PEER_TOPK))
        cand, expert = _pair_candidates(tops[0][0], tops[0][1], tops[1][0], tops[1][1])
        n = cand.shape[0]
        rows = lax.broadcasted_iota(jnp.int32, cand.shape, 0)
        top, eid = [], []
        for _ in range(PEER_TOPK):
            m = jnp.max(cand, axis=0, keepdims=True)
            r = jnp.min(jnp.where(cand == m, rows, n), axis=0, keepdims=True)
            sel = rows == r
            top.append(m)
            eid.append(jnp.max(jnp.where(sel, expert, -1), axis=0, keepdims=True))
            cand = jnp.where(sel, NEG_INF, cand)
        top = jnp.concatenate(top, axis=0)
        ex = jnp.exp(top - top[0:1])
        r0 = pl.multiple_of(hh * PEER_TOPK, PEER_TOPK)
        gw_ref[pl.ds(r0, PEER_TOPK), :] = ex / jnp.sum(ex, axis=0, keepdims=True)
        e_ref[pl.ds(r0, PEER_TOPK), :] = jnp.concatenate(eid, axis=0)
        return carry

    lax.fori_loop(0, PEER_HEADS, head, 0)


def _peer_query(x, mods, gain, wq_b, keys_b, tile_row, n_tokens):
    d = x.shape[1]
    tm = PEER_Q_TILE
    n_sel = PEER_HEADS * PEER_TOPK
    kd = keys_b.shape[2]
    return pl.pallas_call(
        _peer_q_kernel,
        grid=(n_tokens // tm,),
        in_specs=[pl.BlockSpec((tm, d), lambda i: (i, 0)),
                  pl.BlockSpec((1,) + mods.shape[1:], lambda i: (tile_row(i), 0, 0)),
                  _resident(gain.shape), _resident(wq_b.shape), _resident(keys_b.shape)],
        out_specs=[pl.BlockSpec((tm, d), lambda i: (i, 0)),
                   pl.BlockSpec((n_sel, tm), lambda i: (0, i)),
                   pl.BlockSpec((n_sel, tm), lambda i: (0, i))],
        out_shape=[jax.ShapeDtypeStruct((n_tokens, d), F32),
                   jax.ShapeDtypeStruct((n_sel, n_tokens), jnp.int32),
                   jax.ShapeDtypeStruct((n_sel, n_tokens), F32)],
        scratch_shapes=[pltpu.VMEM((2 * PEER_HEADS, tm, kd), BF16)],
        compiler_params=_cparams("arbitrary"),
        name="peer_query",
    )(x, mods, gain, wq_b, keys_b)


FOLD_SHIFTS = (1, 2, 4)


def _fold_pairs(vs, shift):
    sub = lax.broadcasted_iota(jnp.int32, vs[0].shape, 0)
    keep = (sub % (2 * shift)) < shift
    return [jnp.where(keep, vs[i], vs[i + 1]) + pltpu.roll(jnp.where(keep, vs[i + 1], vs[i]), shift, axis=0)
            for i in range(0, len(vs), 2)]


def _trace_fold_rows():
    vs = [[i] * SUBLANES for i in range(SUBLANES)]
    for shift in FOLD_SHIFTS:
        vs = [[vs[i][r] if (r % (2 * shift)) < shift else vs[i + 1][r] for r in range(SUBLANES)]
              for i in range(0, len(vs), 2)]
    return vs[0]


_SLOT_PERM = _trace_fold_rows()


def _pack_rows(tab):
    n = tab.shape[0]
    t = tab.astype(BF16).reshape(n, SUBLANES // 2, 2, LANES).transpose(0, 1, 3, 2)
    return lax.bitcast_convert_type(t, jnp.uint32).reshape(n * (SUBLANES // 2), LANES)


def _gather_row(tab_ref, row4):
    words = tab_ref[pl.ds(pl.multiple_of(row4, SUBLANES // 2), SUBLANES // 2), :]
    return pltpu.bitcast(words, BF16).astype(F32)


def _split3(v):
    p0 = v.astype(BF16)
    r = v - p0.astype(F32)
    p1 = r.astype(BF16)
    p2 = (r - p1.astype(F32)).astype(BF16)
    return p0, p1, p2


def _peer_u_kernel(*refs):
    idx_refs = refs[:PEER_IDX_PARTS]
    x_ref, tab_ref, g_ref, fold_ref, o_ref, a_scr, at_scr = refs[PEER_IDX_PARTS:PEER_IDX_PARTS + 7]
    stages = refs[PEER_IDX_PARTS + 7:]
    tg, n_sel = g_ref.shape
    group = len(stages)
    per_part = n_sel // PEER_IDX_PARTS
    mxu_parts = PEER_U_MXU_PARTS
    vpu_parts = PEER_IDX_PARTS - mxu_parts
    n_mxu = mxu_parts * per_part
    n_vpu = n_sel - n_mxu
    half = SUBLANES // 2
    n_rows = n_mxu * SUBLANES
    rows = lax.broadcasted_iota(jnp.int32, (SUBLANES, n_rows), 0)
    cols = lax.broadcasted_iota(jnp.int32, (SUBLANES, n_rows), 1)
    same_row = rows == (cols & (SUBLANES - 1))
    lane = lax.broadcasted_iota(jnp.int32, (n_vpu, LANES), 1)
    at_scr[...] = jnp.zeros_like(at_scr)

    def stage_tiles(t, s_ref):
        part_rows = [r.at[pl.ds(t * per_part, per_part)] for r in idx_refs[:mxu_parts]]
        for j in range(per_part):
            for q in range(mxu_parts):
                k = q * per_part + j
                s_ref[k * half:(k + 1) * half, :] = tab_ref[pl.ds(pl.multiple_of(part_rows[q][j], half), half), :]

    def row_dots(t, s_ref):
        tiles = pltpu.bitcast(s_ref[...], BF16)
        xs = jnp.concatenate(_split3(x_ref[t]), axis=0)
        xs = jnp.concatenate([xs, jnp.zeros((SUBLANES, LANES), BF16)], axis=0)
        dots = lax.dot_general(xs, tiles, (((1,), (1,)), ((), ())), preferred_element_type=F32)
        per_piece = [jnp.where(same_row, dots[p * SUBLANES:(p + 1) * SUBLANES], 0.0) for p in range(3)]
        return (per_piece[0] + per_piece[1]) + per_piece[2]

    def lane_partials(t):
        xt = x_ref[t]
        part_rows = [r.at[pl.ds(t * per_part, per_part)] for r in idx_refs[mxu_parts:]]
        prods = [None] * n_vpu
        for j in range(per_part):
            for q in range(vpu_parts):
                f, s = divmod(q * per_part + j, SUBLANES)
                prods[f * SUBLANES + _SLOT_PERM[s]] = _gather_row(tab_ref, part_rows[q][j]) * xt
        for shift in FOLD_SHIFTS:
            prods = _fold_pairs(prods, shift)
        return jnp.concatenate(prods, axis=0)

    def step(i, carry):
        t0 = pl.multiple_of(i * group, group)
        for j in range(group):
            stage_tiles(t0 + j, stages[j])
        at = at_scr[0:n_vpu, :]
        for j in range(group):
            col = jnp.sum(lane_partials(t0 + j), axis=1, keepdims=True)
            at = jnp.where(lane == t0 + j, col, at)
        at_scr[0:n_vpu, :] = at
        parts = [None] * group
        for j in range(group):
            parts[_SLOT_PERM[j]] = row_dots(t0 + j, stages[j])
        for shift in FOLD_SHIFTS:
            parts = _fold_pairs(parts, shift)
        per_token = parts[0]
        pieces = jnp.concatenate(_split3(per_token) + (jnp.zeros((SUBLANES, n_rows), BF16),), axis=0)
        a = jnp.dot(pieces, fold_ref[...], preferred_element_type=F32)
        a_scr[pl.ds(t0, group), :] = (a[0:SUBLANES] + a[SUBLANES:2 * SUBLANES]) + a[2 * SUBLANES:3 * SUBLANES]
        return carry

    lax.fori_loop(0, tg // group, step, 0)
    a_all = jnp.concatenate([a_scr[...], at_scr[...].T[:, 0:n_vpu]], axis=1)
    o_ref[...] = g_ref[...] * jax.nn.gelu(a_all)


def _peer_u(idx_parts, h3, tab, g_tm, fold):
    t, n_sel = g_tm.shape
    tg = PEER_U_TILE
    assert tg == LANES
    n_mxu = PEER_U_MXU_PARTS * n_sel // PEER_IDX_PARTS
    per_tile = tg * n_sel // PEER_IDX_PARTS
    stage = pltpu.VMEM((n_mxu * SUBLANES // 2, LANES), jnp.uint32)
    return pl.pallas_call(
        _peer_u_kernel,
        grid=(t // tg,),
        in_specs=[pl.BlockSpec((per_tile,), lambda i: (i,), memory_space=pltpu.SMEM) for _ in idx_parts]
        + [pl.BlockSpec((tg,) + h3.shape[1:], lambda i: (i, 0, 0)),
           pl.BlockSpec(tab.shape, lambda i: (0, 0), pipeline_mode=pl.Buffered(1)),
           pl.BlockSpec((tg, n_sel), lambda i: (i, 0)),
           _resident(fold.shape)],
        out_specs=pl.BlockSpec((tg, n_sel), lambda i: (i, 0)),
        out_shape=jax.ShapeDtypeStruct((t, n_sel), F32),
        scratch_shapes=[pltpu.VMEM((tg, n_mxu), F32), pltpu.VMEM((LANES, LANES), F32)] + [stage] * SUBLANES,
        compiler_params=_cparams("arbitrary"),
        name="peer_u",
    )(*idx_parts, h3, tab, g_tm, fold)


def _split_index(idx_tm, parts):
    t, n = idx_tm.shape
    split = idx_tm.reshape(t, parts, n // parts)
    return [split[:, q, :].reshape(-1) for q in range(parts)]


def _peer_v_kernel(*refs):
    idx_refs = refs[:PEER_IDX_PARTS]
    w_ref, tab_ref, o_ref, wb_scr = refs[PEER_IDX_PARTS:]
    n_sel, tg = w_ref.shape
    per_part = n_sel // PEER_IDX_PARTS
    n_acc = 4
    for t in range(tg):
        wb_scr[t] = jnp.broadcast_to(w_ref[:, t:t + 1], (n_sel, LANES))

    def token(t, carry):
        rows = [r.at[pl.ds(t * per_part, per_part)] for r in idx_refs]
        acc = [None] * n_acc
        for j in range(per_part):
            for q in range(PEER_IDX_PARTS):
                k = q * per_part + j
                term = _gather_row(tab_ref, rows[q][j]) * wb_scr[t, k:k + 1, :]
                a = k % n_acc
                acc[a] = term if acc[a] is None else acc[a] + term
        o_ref[t] = (acc[0] + acc[1]) + (acc[2] + acc[3])
        return carry

    lax.fori_loop(0, tg, token, 0)


def _peer_v(idx_parts, w_km, tab):
    n_sel, t = w_km.shape
    tg = PEER_V_TILE
    per_tile = tg * n_sel // PEER_IDX_PARTS
    return pl.pallas_call(
        _peer_v_kernel,
        grid=(t // tg,),
        in_specs=[pl.BlockSpec((per_tile,), lambda i: (i,), memory_space=pltpu.SMEM) for _ in idx_parts]
        + [pl.BlockSpec((n_sel, tg), lambda i: (0, i)),
           pl.BlockSpec(tab.shape, lambda i: (0, 0), pipeline_mode=pl.Buffered(1))],
        out_specs=pl.BlockSpec((tg, SUBLANES, LANES), lambda i: (i, 0, 0)),
        out_shape=jax.ShapeDtypeStruct((t, SUBLANES, LANES), F32),
        scratch_shapes=[pltpu.VMEM((tg, n_sel, LANES), F32)],
        compiler_params=_cparams("arbitrary"),
        name="peer_v",
    )(*idx_parts, w_km, tab)


def _residual_kernel(x_ref, y_ref, mod_ref, o_ref):
    for c in range(SUBLANES):
        cols = slice(c * LANES, (c + 1) * LANES)
        o_ref[:, cols] = x_ref[:, cols] + mod_ref[0, 5:6, cols] * y_ref[:, c, :]


def _residual(x, y, mods, tile_row, n_tokens):
    d = x.shape[1]
    tm = TOKEN_TILE
    row = lambda i: (i, 0)
    return pl.pallas_call(
        _residual_kernel,
        grid=(n_tokens // tm,),
        in_specs=[pl.BlockSpec((tm, d), row), pl.BlockSpec((tm, SUBLANES, LANES), lambda i: (i, 0, 0)),
                  pl.BlockSpec((1,) + mods.shape[1:], lambda i: (tile_row(i), 0, 0))],
        out_specs=pl.BlockSpec((tm, d), row),
        out_shape=jax.ShapeDtypeStruct((n_tokens, d), F32),
        compiler_params=_cparams("arbitrary"),
        name="residual",
    )(x, y, mods)


def _dft_tables(n):
    j = jnp.arange(n, dtype=jnp.int32)
    ang = (2.0 * math.pi / n) * ((j[:, None] * j[None, :]) % n).astype(F32)
    return jnp.cos(ang), jnp.sin(ang)


def _rope_tables(n):
    f = HEAD_DIM // 4
    rows = n // GRID_W
    row = jnp.repeat(jnp.arange(rows), GRID_W).astype(F32)
    col = jnp.tile(jnp.arange(GRID_W), rows).astype(F32)
    inv = ROPE_THETA ** (-jnp.arange(f, dtype=F32) / f)
    ar, ac = row[:, None] * inv, col[:, None] * inv
    cos = jnp.concatenate([jnp.cos(ar), jnp.cos(ar), jnp.cos(ac), jnp.cos(ac)], axis=1)
    sin = jnp.concatenate([-jnp.sin(ar), jnp.sin(ar), -jnp.sin(ac), jnp.sin(ac)], axis=1)
    return cos, sin


def _block_diag(blocks):
    g, r, c = blocks.shape
    out = jnp.zeros((g * r, g * c), blocks.dtype)
    for i in range(g):
        out = out.at[i * r:(i + 1) * r, i * c:(i + 1) * c].set(blocks[i])
    return out


def _seq_dft(fg, n_batch, n_seq, cs):
    w2 = fg.shape[1]
    w = w2 // 2
    g = fg.reshape(n_batch, n_seq, 2, w).transpose(2, 1, 0, 3).reshape(2 * n_seq, n_batch * w)
    tm = min(512, n_seq)
    re = _matmul(cs, g, tm, min(512, n_batch * w))
    return re.reshape(n_seq, n_batch, w).transpose(1, 0, 2).reshape(n_batch * n_seq, w)


def kernel(x, c, ctx, c_ctx, ada_w, ada_b, norm_mix, w_in, fourier_w, pool_w, pool_scale, pool_proj,
           q_norm, k_norm, attn_proj, w_out, norm_ffn, peer_wq, peer_keys, peer_u, peer_v):
    nb, n_lat, d = x.shape
    n_ctx = ctx.shape[1]
    depth = ada_w.shape[0]
    fw = fourier_w.shape[1]
    pw = pool_proj.shape[1]
    qw = attn_proj.shape[1]
    n_heads = qw // HEAD_DIM
    n_kv = n_heads // GQA_GROUP
    kw = n_kv * HEAD_DIM
    widths = (fw, pw, qw, kw)
    t_lat, t_ctx = nb * n_lat, nb * n_ctx
    t_all = t_lat + t_ctx
    n_exp = peer_u.shape[1]
    n_sel = PEER_HEADS * PEER_TOPK
    assert n_lat % TOKEN_TILE == 0 and n_ctx % TOKEN_TILE == 0 and t_all % PEER_U_TILE == 0
    assert n_lat % PEER_Q_TILE == 0 and n_ctx % PEER_Q_TILE == 0
    assert d == SUBLANES * LANES

    def rows_for(tm):
        return lambda i: jnp.where(i < t_lat // tm, i // (n_lat // tm), nb)

    tile_row = rows_for(TOKEN_TILE)

    rows = -(-(nb + 1) // SUBLANES) * SUBLANES
    cvec = jnp.zeros((rows, d), F32).at[:nb].set(c).at[nb].set(c_ctx)
    mods_all = _adaln(cvec, ada_w, ada_b).reshape(depth, rows, 6, d)

    gd = fw // FOURIER_GROUPS
    cc, sc = _dft_tables(gd)
    norm = 1.0 / math.sqrt(gd)
    eye = jnp.eye(FOURIER_GROUPS, dtype=F32)
    dft_c = jnp.concatenate([jnp.kron(eye, cc), jnp.kron(eye, sc)], axis=1) * norm
    dft_c = dft_c.astype(BF16)

    def seq_tables(n):
        cl, sl = _dft_tables(n)
        return (jnp.concatenate([cl, -sl], axis=1) * (1.0 / math.sqrt(n))).astype(BF16)

    cs_lat, cs_ctx = seq_tables(n_lat), seq_tables(n_ctx)
    cos, sin = _rope_tables(n_lat)

    n_mxu = PEER_U_MXU_PARTS * n_sel // PEER_IDX_PARTS
    fold = jnp.repeat(jnp.eye(n_mxu, dtype=BF16), SUBLANES, axis=0)

    stream = jnp.concatenate([x.reshape(t_lat, d), ctx.reshape(t_ctx, d)], axis=0)

    for l in range(depth):
        last = l == depth - 1
        mods = mods_all[l]
        w_in_b = w_in[l].astype(BF16)
        fg, p, q, k, v, gates = _in_proj(stream, mods, norm_mix[l][None, :], w_in_b, dft_c, tile_row, widths)

        re_lat = _seq_dft(fg[:t_lat], nb, n_lat, cs_lat)
        w_bd = _block_diag(pool_w[l]).astype(BF16)
        scale = pool_scale[l][None, :]
        mx_lat = _pool(p[:t_lat].reshape(nb, n_lat, pw), w_bd, scale).reshape(t_lat, pw)
        def heads(a, n, h):
            return a.reshape(nb, n, h, HEAD_DIM).transpose(0, 2, 1, 3)
        q_lat, q_ctx = heads(q[:t_lat], n_lat, n_heads), heads(q[t_lat:], n_ctx, n_heads)
        k_lat, k_ctx = heads(k[:t_lat], n_lat, n_kv), heads(k[t_lat:], n_ctx, n_kv)
        v_lat, v_ctx = heads(v[:t_lat], n_lat, n_kv), heads(v[t_lat:], n_ctx, n_kv)
        qg, kg = q_norm[l][None, :], k_norm[l][None, :]
        at_lat = _attention(q_lat, jnp.concatenate([k_lat, k_ctx], axis=2),
                            jnp.concatenate([v_lat, v_ctx], axis=2), cos, sin, qg, kg, n_lat)
        at_lat = at_lat.transpose(0, 2, 1, 3).reshape(t_lat, qw)

        if last:
            re, mixed, attn, n_tok = re_lat, mx_lat, at_lat, t_lat
        else:
            re_ctx = _seq_dft(fg[t_lat:], nb, n_ctx, cs_ctx)
            mx_ctx = _pool(p[t_lat:].reshape(nb, n_ctx, pw), w_bd, scale).reshape(t_ctx, pw)
            at_ctx = _attention(q_ctx, k_ctx, v_ctx, cos, sin, qg, kg, 0)
            at_ctx = at_ctx.transpose(0, 2, 1, 3).reshape(t_ctx, qw)
            re = jnp.concatenate([re_lat, re_ctx], axis=0)
            mixed = jnp.concatenate([mx_lat, mx_ctx], axis=0)
            attn = jnp.concatenate([at_lat, at_ctx], axis=0)
            n_tok = t_all

        x1 = _merge(re, mixed, attn, gates, stream, mods,
                    fourier_w[l].astype(BF16), pool_proj[l].astype(BF16),
                    attn_proj[l].astype(BF16), w_out[l].astype(BF16), tile_row, n_tok)

        keys_b = peer_keys[l].reshape(2 * PEER_HEADS, PEER_KEYS, -1).astype(BF16)
        h2, e_t, g_t = _peer_query(x1, mods, norm_ffn[l][None, :], peer_wq[l].astype(BF16), keys_b,
                                   rows_for(PEER_Q_TILE), n_tok)
        idx_parts = _split_index(e_t.T * (SUBLANES // 2), PEER_IDX_PARTS)
        w_tm = _peer_u(idx_parts, h2.reshape(n_tok, SUBLANES, LANES), _pack_rows(peer_u[l]), g_t.T, fold)
        y = _peer_v(idx_parts, w_tm.T, _pack_rows(peer_v[l]))
        stream = _residual(x1, y, mods, tile_row, n_tok)

    return stream[:t_lat].reshape(nb, n_lat, d)
```
